```python
import math
import jax, jax.numpy as jnp
from jax import lax
import numpy as np

D_MODEL = 1024
BATCH = 8
SEQ = 4096
DEPTH = 1
DEC_BATCH = 128
DEC_SEQ = 8
PAST_LEN = 8192
PAGE_SIZE = 128

D_MIX = D_MODEL
D_POOL = D_MIX // 2
D_ATTN = D_MIX - D_POOL
POOL_WINDOWS = (2, 4, 8, 16)
N_POOL_GROUPS = len(POOL_WINDOWS)
POOL_GROUP_DIM = D_POOL // N_POOL_GROUPS
POOL_BUF = max(POOL_WINDOWS) - 1
ATTN_HEAD_DIM = 64
N_ATTN_HEADS = D_ATTN // (2 * ATTN_HEAD_DIM)
ATTN_SCALE = ATTN_HEAD_DIM ** -0.5
NUM_BUCKETS = 32
MAX_DISTANCE = 128
MAX_EXACT = NUM_BUCKETS // 2
QBLOCK = 128
NEG_INF = -1e30
LN_EPS = 1e-5
SUBLN_EPS = 1e-5
DEEPNORM_ALPHA = (2 * DEPTH) ** 0.25
DEEPNORM_BETA = (8 * DEPTH) ** -0.25
D_IN = 2 * D_POOL + 4 * D_ATTN
V_COL_START = 2 * D_POOL + 2 * D_ATTN

kernel_name = "hymba_pool_diffattn_decoder_step"


def _lambda_init(layer):
    return 0.8 - 0.6 * math.exp(-0.3 * layer)


def _bucket(dist):
    is_small = dist < MAX_EXACT
    d = jnp.maximum(dist, 1).astype(jnp.float32)
    large = MAX_EXACT + (jnp.log(d / MAX_EXACT) / math.log(MAX_DISTANCE / MAX_EXACT)
                         * (NUM_BUCKETS - MAX_EXACT)).astype(jnp.int32)
    large = jnp.minimum(large, NUM_BUCKETS - 1)
    return jnp.where(is_small, dist, large)


def _rel_bias(qpos, kpos, table):
    dist = qpos[:, None] - kpos[None, :]
    b = table.astype(jnp.float32)[_bucket(jnp.maximum(dist, 0))]
    b = jnp.transpose(b, (2, 0, 1))
    return jnp.where((dist >= 0)[None], b, NEG_INF)


def _diff_attend(q, k, v, bias, lam):
    logits = jnp.einsum('bqhcd,bkhcd->bhcqk', q.astype(jnp.float32), k.astype(jnp.float32)) * ATTN_SCALE
    logits = logits + bias[None, :, None]
    p = jax.nn.softmax(logits, axis=-1)
    a = p[:, :, 0] - lam * p[:, :, 1]
    return jnp.einsum('bhqk,bkhe->bqhe', a, v.astype(jnp.float32))


def _project(x, w_in):
    B, L, _ = x.shape
    p = jnp.einsum('bld,de->ble', x, w_in)
    pool_u = p[..., :D_POOL]
    pool_z = p[..., D_POOL:2 * D_POOL]
    o = 2 * D_POOL
    q = p[..., o:o + D_ATTN].reshape(B, L, N_ATTN_HEADS, 2, ATTN_HEAD_DIM)
    k = p[..., o + D_ATTN:o + 2 * D_ATTN].reshape(B, L, N_ATTN_HEADS, 2 * ATTN_HEAD_DIM)
    v = p[..., o + 2 * D_ATTN:o + 3 * D_ATTN].reshape(B, L, N_ATTN_HEADS, 2 * ATTN_HEAD_DIM)
    attn_z = p[..., o + 3 * D_ATTN:]
    return pool_u, pool_z, q, k, v, attn_z


def _pool_mix(u_ext, pos0, pool_w, pool_scale):
    L = u_ext.shape[1] - POOL_BUF
    u32 = u_ext.astype(jnp.float32)
    cs = jnp.pad(jnp.cumsum(u32, axis=1), ((0, 0), (1, 0), (0, 0)))
    pos = pos0 + jnp.arange(L)
    u_new = u32[:, POOL_BUF:]
    outs = []
    for g, w in enumerate(POOL_WINDOWS):
        sl = slice(g * POOL_GROUP_DIM, (g + 1) * POOL_GROUP_DIM)
        wsum = cs[:, POOL_BUF + 1:, sl] - cs[:, POOL_BUF + 1 - w:POOL_BUF + 1 - w + L, sl]
        cnt = jnp.minimum(pos + 1, w).astype(jnp.float32)[None, :, None]
        d = wsum / cnt - u_new[..., sl]
        outs.append(jnp.einsum('blc,cd->bld', d, pool_w[g].astype(jnp.float32)))
    return jnp.concatenate(outs, axis=-1) * pool_scale.astype(jnp.float32)


def _finish(x, pool_y, pool_z, attn_o, attn_z, lam_init, subln_g, w_out, ln_g, ln_b):
    B, L, _ = x.shape
    o = attn_o * lax.rsqrt(jnp.mean(attn_o * attn_o, axis=-1, keepdims=True) + SUBLN_EPS)
    o = (o * subln_g.astype(jnp.float32) * (1.0 - lam_init)).reshape(B, L, D_ATTN)
    gated = jnp.concatenate([pool_y * jax.nn.silu(pool_z.astype(jnp.float32)),
                             o * jax.nn.silu(attn_z.astype(jnp.float32))], axis=-1).astype(x.dtype)
    h = jnp.einsum('ble,ed->bld', gated, w_out)
    r = DEEPNORM_ALPHA * x.astype(jnp.float32) + h.astype(jnp.float32)
    mu = jnp.mean(r, axis=-1, keepdims=True)
    var = jnp.mean(jnp.square(r - mu), axis=-1, keepdims=True)
    y = (r - mu) * lax.rsqrt(var + LN_EPS) * ln_g.astype(jnp.float32) + ln_b.astype(jnp.float32)
    return y.astype(x.dtype)


def setup_inputs(seed: int = 0) -> dict:
    key = jax.random.key(seed)
    ks = jax.random.split(key, 18)
    n_pages = PAST_LEN // PAGE_SIZE
    n_pool_pages = (DEC_BATCH * n_pages * 5) // 4
    hd2 = 2 * ATTN_HEAD_DIM
    f32 = jnp.float32
    x_prompt = jax.random.normal(ks[0], (BATCH, SEQ, D_MODEL), f32)
    x_sample = jax.random.normal(ks[1], (DEC_BATCH, DEC_SEQ, D_MODEL), f32)
    cache_k = jax.random.normal(ks[2], (DEPTH, n_pool_pages, PAGE_SIZE, N_ATTN_HEADS, hd2), f32)
    cache_v = jax.random.normal(ks[3], (DEPTH, n_pool_pages, PAGE_SIZE, N_ATTN_HEADS, hd2), f32) * DEEPNORM_BETA
    state_pool = jax.random.normal(ks[4], (DEPTH, DEC_BATCH, POOL_BUF, D_POOL), f32)
    page_table = jax.random.permutation(ks[5], n_pool_pages)[:DEC_BATCH * n_pages].reshape(DEC_BATCH, n_pages).astype(jnp.int32)
    col_scale = jnp.ones((D_IN,), f32).at[V_COL_START:V_COL_START + D_ATTN].set(DEEPNORM_BETA)
    w_in = jax.random.normal(ks[6], (DEPTH, D_MODEL, D_IN), f32) * (D_MODEL ** -0.5) * col_scale
    pool_w = jax.random.normal(ks[7], (DEPTH, N_POOL_GROUPS, POOL_GROUP_DIM, POOL_GROUP_DIM), f32) * (POOL_GROUP_DIM ** -0.5)
    pool_scale = 1.0 + 0.02 * jax.random.normal(ks[8], (DEPTH, D_POOL), f32)
    lambda_q1 = 0.1 * jax.random.normal(ks[9], (DEPTH, ATTN_HEAD_DIM), f32)
    lambda_k1 = 0.1 * jax.random.normal(ks[10], (DEPTH, ATTN_HEAD_DIM), f32)
    lambda_q2 = 0.1 * jax.random.normal(ks[11], (DEPTH, ATTN_HEAD_DIM), f32)
    lambda_k2 = 0.1 * jax.random.normal(ks[12], (DEPTH, ATTN_HEAD_DIM), f32)
    subln_g = 1.0 + 0.02 * jax.random.normal(ks[13], (DEPTH, hd2), f32)
    rel_bias = 0.5 * jax.random.normal(ks[14], (NUM_BUCKETS, N_ATTN_HEADS), f32)
    w_out = jax.random.normal(ks[15], (DEPTH, D_MIX, D_MODEL), f32) * (D_MIX ** -0.5) * DEEPNORM_BETA
    ln_g = 1.0 + 0.02 * jax.random.normal(ks[16], (DEPTH, D_MODEL), f32)
    ln_b = 0.02 * jax.random.normal(ks[17], (DEPTH, D_MODEL), f32)
    return {"x_prompt": x_prompt, "x_sample": x_sample, "cache_k": cache_k, "cache_v": cache_v,
            "state_pool": state_pool, "page_table": page_table, "w_in": w_in, "pool_w": pool_w,
            "pool_scale": pool_scale, "lambda_q1": lambda_q1, "lambda_k1": lambda_k1,
            "lambda_q2": lambda_q2, "lambda_k2": lambda_k2, "subln_g": subln_g, "rel_bias": rel_bias,
            "w_out": w_out, "ln_g": ln_g, "ln_b": ln_b}


def reference(x_prompt, x_sample, cache_k, cache_v, state_pool, page_table, w_in, pool_w, pool_scale,
              lambda_q1, lambda_k1, lambda_q2, lambda_k2, subln_g, rel_bias, w_out, ln_g, ln_b):
    B, S, _ = x_prompt.shape
    DB, L, _ = x_sample.shape
    n_pages = page_table.shape[1]
    page = cache_k.shape[2]
    past_len = n_pages * page
    nb = S // QBLOCK
    kpos_p = jnp.arange(S)
    qpos_s = past_len + jnp.arange(L)
    kpos_s = jnp.arange(past_len + L)
    bias_s = _rel_bias(qpos_s, kpos_s, rel_bias)

    xp, xs = x_prompt, x_sample
    kp_l, vp_l, pp_l, ks_l, vs_l, ps_l = [], [], [], [], [], []
    for layer in range(DEPTH):
        lam_init = _lambda_init(layer)
        lam = (jnp.exp(jnp.sum(lambda_q1[layer].astype(jnp.float32) * lambda_k1[layer].astype(jnp.float32)))
               - jnp.exp(jnp.sum(lambda_q2[layer].astype(jnp.float32) * lambda_k2[layer].astype(jnp.float32)))
               + lam_init)

        pu, pz, q, k, v, az = _project(xp, w_in[layer])
        u_ext = jnp.concatenate([jnp.zeros((B, POOL_BUF, D_POOL), pu.dtype), pu], axis=1)
        pool_y = _pool_mix(u_ext, 0, pool_w[layer], pool_scale[layer])
        k5 = k.reshape(B, S, N_ATTN_HEADS, 2, ATTN_HEAD_DIM)
        qb = jnp.transpose(q.reshape(B, nb, QBLOCK, N_ATTN_HEADS, 2, ATTN_HEAD_DIM), (1, 0, 2, 3, 4, 5))

        def _block(args, k5=k5, v=v, lam=lam):
            qi, i = args
            qpos = i * QBLOCK + jnp.arange(QBLOCK)
            return _diff_attend(qi, k5, v, _rel_bias(qpos, kpos_p, rel_bias), lam)

        o = lax.map(_block, (qb, jnp.arange(nb)))
        o = jnp.transpose(o, (1, 0, 2, 3, 4)).reshape(B, S, N_ATTN_HEADS, 2 * ATTN_HEAD_DIM)
        yp = _finish(xp, pool_y, pz, o, az, lam_init, subln_g[layer], w_out[layer], ln_g[layer], ln_b[layer])
        kp_l.append(k)
        vp_l.append(v)
        pp_l.append(u_ext[:, -POOL_BUF:])

        su, sz, sq, sk, sv, saz = _project(xs, w_in[layer])
        s_ext = jnp.concatenate([state_pool[layer].astype(su.dtype), su], axis=1)
        s_pool_y = _pool_mix(s_ext, past_len, pool_w[layer], pool_scale[layer])
        past_k = cache_k[layer][page_table].reshape(DB, past_len, N_ATTN_HEADS, 2 * ATTN_HEAD_DIM)
        past_v = cache_v[layer][page_table].reshape(DB, past_len, N_ATTN_HEADS, 2 * ATTN_HEAD_DIM)
        k_all = jnp.concatenate([past_k.astype(sk.dtype), sk], axis=1).reshape(DB, past_len + L, N_ATTN_HEADS, 2, ATTN_HEAD_DIM)
        v_all = jnp.concatenate([past_v.astype(sv.dtype), sv], axis=1)
        so = _diff_attend(sq, k_all, v_all, bias_s, lam)
        ys = _finish(xs, s_pool_y, sz, so, saz, lam_init, subln_g[layer], w_out[layer], ln_g[layer], ln_b[layer])
        ks_l.append(sk)
        vs_l.append(sv)
        ps_l.append(s_ext[:, -POOL_BUF:])

        xp, xs = yp, ys

    return (xp, xs, jnp.stack(kp_l), jnp.stack(vp_l), jnp.stack(pp_l),
            jnp.stack(ks_l), jnp.stack(vs_l), jnp.stack(ps_l))
```

```python
import functools
import math

import jax
import jax.numpy as jnp
from jax import lax
from jax.experimental import pallas as pl
from jax.experimental.pallas import tpu as pltpu

F32 = jnp.float32
BF16 = jnp.bfloat16

POOL_WINDOWS = (2, 4, 8, 16)
POOL_BUF = max(POOL_WINDOWS) - 1
HEAD_DIM = 64
V_DIM = 2 * HEAD_DIM
ATTN_SCALE = HEAD_DIM ** -0.5
NUM_BUCKETS = 32
MAX_DISTANCE = 128
MAX_EXACT = NUM_BUCKETS // 2
NEG_INF = -1e30
LN_EPS = 1e-5
SUBLN_EPS = 1e-5

LANES = 128
SUBLANES = 8
HALO = 2 * SUBLANES
VMEM_LIMIT_BYTES = 56 * 1024 * 1024

PROJ_ROWS = 512
ATTN_TILE = 256
DECODE_PAGES = 8
NT_DIMS = (((1,), (1,)), ((), ()))


def _lambda_init(layer):
    return 0.8 - 0.6 * math.exp(-0.3 * layer)


def _silu(z):
    return z / (1.0 + jnp.exp(-z))


def _lam_value(lq1, lk1, lq2, lk2, lam_init):
    a = jnp.sum(lq1 * lk1, axis=-1, keepdims=True)
    b = jnp.sum(lq2 * lk2, axis=-1, keepdims=True)
    return jnp.exp(a) - jnp.exp(b) + lam_init


def _compiler_params(n_axes):
    return pltpu.CompilerParams(dimension_semantics=("arbitrary",) * n_axes,
                                vmem_limit_bytes=VMEM_LIMIT_BYTES)


def _pool_branch(u, window_sum, cnt_of, pw_ref, ps_ref):
    outs = []
    for g, w in enumerate(POOL_WINDOWS):
        sl = slice(g * LANES, (g + 1) * LANES)
        d = window_sum(g, w) / cnt_of(w) - u[:, sl]
        outs.append(jnp.dot(d.astype(BF16), pw_ref[g], preferred_element_type=F32))
    return jnp.concatenate(outs, axis=-1) * ps_ref[...]


def _store_heads_interleaved(dst_ref, val, rows, n_heads):
    for h in range(n_heads):
        dst_ref[pl.ds(h, rows, stride=n_heads), :] = val[:, h * V_DIM:(h + 1) * V_DIM]


def _proj_prompt_kernel(x_ref, w_ref, pw_ref, ps_ref,
                        k_ref, v_ref, q_ref, kb_ref, vb_ref, gp_ref, sz_ref, pp_ref,
                        ext_ref, *, bm, d_pool, d_attn, n_heads):
    s = pl.program_id(1)
    xb = x_ref[...].astype(BF16)
    offs = [0, d_pool, 2 * d_pool, 2 * d_pool + d_attn, 2 * d_pool + 2 * d_attn,
            2 * d_pool + 3 * d_attn, 2 * d_pool + 4 * d_attn]

    def proj(j):
        return jnp.dot(xb, w_ref[:, offs[j]:offs[j + 1]], preferred_element_type=F32)

    pu = proj(0)

    @pl.when(s == 0)
    def _():
        ext_ref[0:HALO, :] = jnp.zeros((HALO, d_pool), F32)

    ext_ref[HALO:HALO + bm, :] = pu
    pos = s * bm + lax.broadcasted_iota(jnp.int32, (bm, 1), 0)

    def window_sum(g, w):
        sl = slice(g * LANES, (g + 1) * LANES)
        acc = pu[:, sl]
        for sh in range(1, w):
            acc = acc + ext_ref[HALO - sh:HALO - sh + bm, sl]
        return acc

    def cnt_of(w):
        return jnp.minimum(pos + 1, w).astype(F32)

    pool_y = _pool_branch(pu, window_sum, cnt_of, pw_ref, ps_ref)
    ext_ref[0:HALO, :] = pu[bm - HALO:, :]

    gp_ref[...] = (pool_y * _silu(proj(1))).astype(BF16)
    q_ref[...] = (proj(2) * ATTN_SCALE).astype(BF16)
    k = proj(3)
    kb_ref[...] = k.astype(BF16)
    _store_heads_interleaved(k_ref, k, bm, n_heads)
    v = proj(4)
    vb_ref[...] = v.astype(BF16)
    _store_heads_interleaved(v_ref, v, bm, n_heads)
    sz_ref[...] = _silu(proj(5))

    @pl.when(s == pl.num_programs(1) - 1)
    def _():
        pp_ref[...] = pu[bm - HALO:, :]


def _proj_prompt(x, w_in, pool_w, pool_scale, d_pool, d_attn, n_heads):
    B, S, D = x.shape
    bm = PROJ_ROWS
    assert S % bm == 0 and bm >= HALO
    d_in = w_in.shape[1]
    row_blk = lambda width: pl.BlockSpec((None, bm, width), lambda b, s: (b, s, 0))
    const2 = lambda shape: pl.BlockSpec(shape, lambda b, s: (0,) * len(shape))
    out_shape = (
        jax.ShapeDtypeStruct((B, S * n_heads, V_DIM), F32),
        jax.ShapeDtypeStruct((B, S * n_heads, V_DIM), F32),
        jax.ShapeDtypeStruct((B, S, d_attn), BF16),
        jax.ShapeDtypeStruct((B, S, d_attn), BF16),
        jax.ShapeDtypeStruct((B, S, d_attn), BF16),
        jax.ShapeDtypeStruct((B, S, d_pool), BF16),
        jax.ShapeDtypeStruct((B, S, d_attn), F32),
        jax.ShapeDtypeStruct((B, HALO, d_pool), F32),
    )
    kv_blk = pl.BlockSpec((None, bm * n_heads, V_DIM), lambda b, s: (b, s, 0))
    out_specs = (kv_blk, kv_blk, row_blk(d_attn), row_blk(d_attn), row_blk(d_attn),
                 row_blk(d_pool), row_blk(d_attn),
                 pl.BlockSpec((None, HALO, d_pool), lambda b, s: (b, 0, 0)))
    return pl.pallas_call(
        functools.partial(_proj_prompt_kernel, bm=bm, d_pool=d_pool, d_attn=d_attn, n_heads=n_heads),
        grid=(B, S // bm),
        in_specs=[row_blk(D), const2((D, d_in)), const2(pool_w.shape), const2(pool_scale.shape)],
        out_specs=out_specs,
        out_shape=out_shape,
        scratch_shapes=[pltpu.VMEM((HALO + bm, d_pool), F32)],
        compiler_params=_compiler_params(2),
        name="proj_prompt",
    )(x, w_in, pool_w, pool_scale)


def _proj_sample_kernel(x_ref, st_ref, w_ref, pw_ref, ps_ref,
                        k_ref, v_ref, q_ref, gp_ref, sz_ref, ps_out_ref,
                        ext_ref, *, nb, L, pos0, d_pool, d_attn, n_heads):
    rows = nb * L
    xb = x_ref[...].astype(BF16)
    offs = [0, d_pool, 2 * d_pool, 2 * d_pool + d_attn, 2 * d_pool + 2 * d_attn,
            2 * d_pool + 3 * d_attn, 2 * d_pool + 4 * d_attn]

    def proj(j):
        return jnp.dot(xb, w_ref[:, offs[j]:offs[j + 1]], preferred_element_type=F32)

    su = proj(0)
    ext_ref[:, HALO - POOL_BUF:HALO, :] = st_ref[...]
    ext_ref[:, HALO:HALO + L, :] = su.reshape(nb, L, d_pool)
    pos = pos0 + lax.rem(lax.broadcasted_iota(jnp.int32, (rows, 1), 0), L)

    def window_sum(g, w):
        sl = slice(g * LANES, (g + 1) * LANES)
        acc = ext_ref[:, HALO:HALO + L, sl]
        for sh in range(1, w):
            acc = acc + ext_ref[:, HALO - sh:HALO - sh + L, sl]
        return acc.reshape(rows, LANES)

    def cnt_of(w):
        return jnp.minimum(pos + 1, w).astype(F32)

    pool_y = _pool_branch(su, window_sum, cnt_of, pw_ref, ps_ref)
    ps_out_ref[...] = ext_ref[:, HALO + L - POOL_BUF:HALO + L, :]

    gp_ref[...] = (pool_y * _silu(proj(1))).astype(BF16)
    q_ref[...] = proj(2) * ATTN_SCALE
    _store_heads_interleaved(k_ref, proj(3), rows, n_heads)
    _store_heads_interleaved(v_ref, proj(4), rows, n_heads)
    sz_ref[...] = _silu(proj(5))


def _proj_sample(x2d, state, w_in, pool_w, pool_scale, L, pos0, d_pool, d_attn, n_heads):
    rows_total, D = x2d.shape
    DB = state.shape[0]
    assert L == SUBLANES and rows_total == DB * L
    rows = min(PROJ_ROWS, rows_total)
    nb = rows // L
    assert rows_total % rows == 0
    d_in = w_in.shape[1]
    row_blk = lambda width: pl.BlockSpec((rows, width), lambda i: (i, 0))
    const = lambda shape: pl.BlockSpec(shape, lambda i: (0,) * len(shape))
    st_blk = pl.BlockSpec((nb, POOL_BUF, d_pool), lambda i: (i, 0, 0))
    kv_blk = pl.BlockSpec((rows * n_heads, V_DIM), lambda i: (i, 0))
    out_shape = (
        jax.ShapeDtypeStruct((rows_total * n_heads, V_DIM), F32),
        jax.ShapeDtypeStruct((rows_total * n_heads, V_DIM), F32),
        jax.ShapeDtypeStruct((rows_total, d_attn), F32),
        jax.ShapeDtypeStruct((rows_total, d_pool), BF16),
        jax.ShapeDtypeStruct((rows_total, d_attn), F32),
        jax.ShapeDtypeStruct((DB, POOL_BUF, d_pool), F32),
    )
    return pl.pallas_call(
        functools.partial(_proj_sample_kernel, nb=nb, L=L, pos0=pos0, d_pool=d_pool, d_attn=d_attn,
                          n_heads=n_heads),
        grid=(rows_total // rows,),
        in_specs=[row_blk(D), st_blk, const((D, d_in)), const(pool_w.shape), const(pool_scale.shape)],
        out_specs=(kv_blk, kv_blk, row_blk(d_attn), row_blk(d_pool), row_blk(d_attn), st_blk),
        out_shape=out_shape,
        scratch_shapes=[pltpu.VMEM((nb, HALO + L, d_pool), F32)],
        compiler_params=_compiler_params(1),
        name="proj_sample",
    )(x2d, state, w_in, pool_w, pool_scale)


def _finish_rows(o_heads, lam_init, sz, gp, x, g_ref, wout_ref, lng_ref, lnb_ref, alpha):
    normed = []
    for o in o_heads:
        ms = jnp.mean(o * o, axis=-1, keepdims=True)
        normed.append(o * lax.rsqrt(ms + SUBLN_EPS) * g_ref[...] * (1.0 - lam_init))
    o = jnp.concatenate(normed, axis=-1) * sz
    gated = jnp.concatenate([gp, o.astype(BF16)], axis=-1)
    h = jnp.dot(gated, wout_ref[...], preferred_element_type=F32)
    r = alpha * x + h
    mu = jnp.mean(r, axis=-1, keepdims=True)
    rc = r - mu
    var = jnp.mean(rc * rc, axis=-1, keepdims=True)
    return rc * lax.rsqrt(var + LN_EPS) * lng_ref[...] + lnb_ref[...]


def _attn_prompt_kernel(q_ref, kb_ref, vb_ref, bias_ref, lq1_ref, lk1_ref, lq2_ref, lk2_ref,
                        gp_ref, sz_ref, x_ref, wout_ref, g_ref, lng_ref, lnb_ref,
                        y_ref, *, T, n_heads, lam_init, alpha):
    qi = pl.program_id(1)
    q = q_ref[...].astype(F32)
    lane = lax.broadcasted_iota(jnp.int32, (T, V_DIM), 1)
    qq = []
    for h in range(n_heads):
        qh = q[:, h * V_DIM:(h + 1) * V_DIM]
        qq.append(jnp.concatenate([jnp.where(lane < HEAD_DIM, qh, 0.0),
                                   jnp.where(lane >= HEAD_DIM, qh, 0.0)], axis=0).astype(BF16))

    n_chain = 2 * n_heads
    init = (tuple(jnp.full((T, 1), -jnp.inf, F32) for _ in range(n_chain)),
            tuple(jnp.zeros((T, 1), F32) for _ in range(n_chain)),
            tuple(jnp.zeros((T, V_DIM), F32) for _ in range(n_chain)))

    def body(j, carry):
        ms, ls, accs = carry
        start = pl.multiple_of(j * T, T)
        kblk = kb_ref[pl.ds(start, T), :]
        vblk = vb_ref[pl.ds(start, T), :]
        bidx = jnp.minimum(qi - j, 2)
        new_m, new_l, new_acc = [], [], []
        for h in range(n_heads):
            kh = kblk[:, h * V_DIM:(h + 1) * V_DIM]
            vh = vblk[:, h * V_DIM:(h + 1) * V_DIM]
            s12 = lax.dot_general(qq[h], kh, NT_DIMS, preferred_element_type=F32)
            bias = bias_ref[bidx, h]
            for c in range(2):
                i = 2 * h + c
                s = s12[c * T:(c + 1) * T, :] + bias
                m_new = jnp.maximum(ms[i], jnp.max(s, axis=-1, keepdims=True))
                a = jnp.exp(ms[i] - m_new)
                p = jnp.exp(s - m_new)
                new_m.append(m_new)
                new_l.append(a * ls[i] + jnp.sum(p, axis=-1, keepdims=True))
                new_acc.append(a * accs[i] + jnp.dot(p.astype(BF16), vh, preferred_element_type=F32))
        return tuple(new_m), tuple(new_l), tuple(new_acc)

    _, ls, accs = lax.fori_loop(0, qi + 1, body, init)

    lam = _lam_value(lq1_ref[...], lk1_ref[...], lq2_ref[...], lk2_ref[...], lam_init)
    o_heads = [accs[2 * h] / ls[2 * h] - lam * (accs[2 * h + 1] / ls[2 * h + 1]) for h in range(n_heads)]
    y_ref[...] = _finish_rows(o_heads, lam_init, sz_ref[...], gp_ref[...], x_ref[...],
                              g_ref, wout_ref, lng_ref, lnb_ref, alpha)


def _attn_prompt(q, kb, vb, bias_tab, lams, gp, sz, x, w_out, subln_g, ln_g, ln_b, lam_init, alpha, n_heads):
    B, S, D = x.shape
    T = ATTN_TILE
    assert S % T == 0 and T >= MAX_DISTANCE
    d_attn = q.shape[-1]
    row_blk = lambda width: pl.BlockSpec((None, T, width), lambda b, i: (b, i, 0))
    seq_blk = pl.BlockSpec((None, S, d_attn), lambda b, i: (b, 0, 0))
    const = lambda a: pl.BlockSpec(a.shape, lambda b, i: (0,) * a.ndim)
    return pl.pallas_call(
        functools.partial(_attn_prompt_kernel, T=T, n_heads=n_heads, lam_init=lam_init, alpha=alpha),
        grid=(B, S // T),
        in_specs=[row_blk(d_attn), seq_blk, seq_blk, const(bias_tab)] + [const(l) for l in lams]
                 + [row_blk(gp.shape[-1]), row_blk(d_attn), row_blk(D), const(w_out), const(subln_g),
                    const(ln_g), const(ln_b)],
        out_specs=row_blk(D),
        out_shape=jax.ShapeDtypeStruct((B, S, D), F32),
        compiler_params=_compiler_params(2),
        name="attn_prompt",
    )(q, kb, vb, bias_tab, *lams, gp, sz, x, w_out, subln_g, ln_g, ln_b)


def _deinterleave_heads(page_ref, rows, n_heads):
    return jnp.concatenate(
        [page_ref[pl.ds(h, rows, stride=n_heads), :].astype(BF16) for h in range(n_heads)], axis=-1)


def _attn_decode_kernel(pt_ref, q_ref, sk_ref, sv_ref, bias_ref, biasn_ref,
                        lq1_ref, lk1_ref, lq2_ref, lk2_ref, *rest,
                        P, page, L, n_heads, lam_init):
    k_refs = rest[:P]
    v_refs = rest[P:2 * P]
    o_ref = rest[2 * P]
    qbd_ref, m_ref, l_ref, acc_ref = rest[2 * P + 1:]
    c = pl.program_id(1)
    d_attn = n_heads * V_DIM

    @pl.when(c == 0)
    def _():
        q = q_ref[...]
        lane = lax.broadcasted_iota(jnp.int32, (L, d_attn), 1)
        rows = []
        for h in range(n_heads):
            for cc in range(2):
                lo = h * V_DIM + cc * HEAD_DIM
                rows.append(jnp.where((lane >= lo) & (lane < lo + HEAD_DIM), q, 0.0))
        qbd_ref[...] = jnp.concatenate(rows, axis=0).astype(BF16)
        m_ref[...] = jnp.full(m_ref.shape, -jnp.inf, F32)
        l_ref[...] = jnp.zeros(l_ref.shape, F32)
        acc_ref[...] = jnp.zeros(acc_ref.shape, F32)

    def update(k_all, v_all, bias):
        s = lax.dot_general(qbd_ref[...], k_all, NT_DIMS, preferred_element_type=F32) + bias
        m_old = m_ref[...]
        m_new = jnp.maximum(m_old, jnp.max(s, axis=-1, keepdims=True))
        a = jnp.exp(m_old - m_new)
        p = jnp.exp(s - m_new)
        l_ref[...] = a * l_ref[...] + jnp.sum(p, axis=-1, keepdims=True)
        acc_ref[...] = a * acc_ref[...] + jnp.dot(p.astype(BF16), v_all, preferred_element_type=F32)
        m_ref[...] = m_new

    k_all = jnp.concatenate([_deinterleave_heads(r, page, n_heads) for r in k_refs], axis=0)
    v_all = jnp.concatenate([_deinterleave_heads(r, page, n_heads) for r in v_refs], axis=0)
    update(k_all, v_all, bias_ref[c])

    @pl.when(c == pl.num_programs(1) - 1)
    def _():
        pad = jnp.zeros((LANES - L, d_attn), BF16)
        update(jnp.concatenate([sk_ref[...].astype(BF16), pad], axis=0),
               jnp.concatenate([sv_ref[...].astype(BF16), pad], axis=0), biasn_ref[...])
        lam = _lam_value(lq1_ref[...], lk1_ref[...], lq2_ref[...], lk2_ref[...], lam_init)
        out = acc_ref[...] / l_ref[...]
        heads = []
        for h in range(n_heads):
            r0 = 2 * h * L
            cols = slice(h * V_DIM, (h + 1) * V_DIM)
            heads.append(out[r0:r0 + L, cols] - lam * out[r0 + L:r0 + 2 * L, cols])
        o_ref[...] = jnp.concatenate(heads, axis=-1)


def _attn_decode(page_table, q, sk, sv, cache_k, cache_v, bias_past, bias_new, lams, lam_init, n_heads):
    DB, L, d_attn = q.shape
    n_pages = page_table.shape[1]
    page = cache_k.shape[1] // n_heads
    P = DECODE_PAGES
    assert n_pages % P == 0 and L == SUBLANES
    n_rows = 2 * n_heads * L
    tok_blk = pl.BlockSpec((None, L, d_attn), lambda b, c, pt: (b, 0, 0))
    const = lambda a: pl.BlockSpec(a.shape, lambda b, c, pt: (0,) * a.ndim)

    def page_blk(i):
        return pl.BlockSpec((None, page * n_heads, V_DIM), lambda b, c, pt: (pt[b, c * P + i], 0, 0))

    grid_spec = pltpu.PrefetchScalarGridSpec(
        num_scalar_prefetch=1,
        grid=(DB, n_pages // P),
        in_specs=[tok_blk, tok_blk, tok_blk, const(bias_past), const(bias_new)] + [const(l) for l in lams]
                 + [page_blk(i) for i in range(P)] + [page_blk(i) for i in range(P)],
        out_specs=tok_blk,
        scratch_shapes=[pltpu.VMEM((n_rows, d_attn), BF16), pltpu.VMEM((n_rows, 1), F32),
                        pltpu.VMEM((n_rows, 1), F32), pltpu.VMEM((n_rows, d_attn), F32)],
    )
    return pl.pallas_call(
        functools.partial(_attn_decode_kernel, P=P, page=page, L=L, n_heads=n_heads, lam_init=lam_init),
        grid_spec=grid_spec,
        out_shape=jax.ShapeDtypeStruct((DB, L, d_attn), F32),
        compiler_params=_compiler_params(2),
        name="attn_decode",
    )(page_table, q, sk, sv, bias_past, bias_new, *lams, *([cache_k] * P), *([cache_v] * P))


def _finish_sample_kernel(o_ref, sz_ref, gp_ref, x_ref, wout_ref, g_ref, lng_ref, lnb_ref, y_ref,
                          *, n_heads, lam_init, alpha):
    o = o_ref[...]
    o_heads = [o[:, h * V_DIM:(h + 1) * V_DIM] for h in range(n_heads)]
    y_ref[...] = _finish_rows(o_heads, lam_init, sz_ref[...], gp_ref[...], x_ref[...],
                              g_ref, wout_ref, lng_ref, lnb_ref, alpha)


def _finish_sample(o, sz, gp, x2d, w_out, subln_g, ln_g, ln_b, lam_init, alpha, n_heads):
    rows_total, D = x2d.shape
    rows = min(PROJ_ROWS, rows_total)
    assert rows_total % rows == 0
    row_blk = lambda a: pl.BlockSpec((rows, a.shape[-1]), lambda i: (i, 0))
    const = lambda a: pl.BlockSpec(a.shape, lambda i: (0,) * a.ndim)
    return pl.pallas_call(
        functools.partial(_finish_sample_kernel, n_heads=n_heads, lam_init=lam_init, alpha=alpha),
        grid=(rows_total // rows,),
        in_specs=[row_blk(o), row_blk(sz), row_blk(gp), row_blk(x2d), const(w_out), const(subln_g),
                  const(ln_g), const(ln_b)],
        out_specs=row_blk(x2d),
        out_shape=jax.ShapeDtypeStruct((rows_total, D), F32),
        compiler_params=_compiler_params(1),
        name="finish_sample",
    )(o, sz, gp, x2d, w_out, subln_g, ln_g, ln_b)


def _bucket(dist):
    is_small = dist < MAX_EXACT
    d = jnp.maximum(dist, 1).astype(F32)
    large = MAX_EXACT + (jnp.log(d / MAX_EXACT) / math.log(MAX_DISTANCE / MAX_EXACT)
                         * (NUM_BUCKETS - MAX_EXACT)).astype(jnp.int32)
    large = jnp.minimum(large, NUM_BUCKETS - 1)
    return jnp.where(is_small, dist, large)


def _bias_of_dist(dist, table):
    b = table.astype(F32)[_bucket(jnp.maximum(dist, 0))]
    return jnp.where((dist >= 0)[None], jnp.transpose(b, (2, 0, 1)), NEG_INF)


def _prompt_bias_tiles(table, T):
    i = jnp.arange(T, dtype=jnp.int32)
    base = i[:, None] - i[None, :]
    return jnp.stack([_bias_of_dist(base + off * T, table) for off in range(3)])


def kernel(x_prompt, x_sample, cache_k, cache_v, state_pool, page_table, w_in, pool_w, pool_scale,
           lambda_q1, lambda_k1, lambda_q2, lambda_k2, subln_g, rel_bias, w_out, ln_g, ln_b):
    B, S, D = x_prompt.shape
    DB, L, _ = x_sample.shape
    depth = w_in.shape[0]
    n_pool_pages, page, n_heads, hd2 = cache_k.shape[1:]
    assert hd2 == V_DIM
    n_pages = page_table.shape[1]
    past_len = n_pages * page
    d_attn = n_heads * V_DIM
    d_pool = pool_scale.shape[-1]
    assert d_pool == len(POOL_WINDOWS) * LANES and w_in.shape[-1] == 2 * d_pool + 4 * d_attn
    alpha = (2 * depth) ** 0.25

    bias_tab = _prompt_bias_tiles(rel_bias, ATTN_TILE)
    qpos_s = past_len + jnp.arange(L, dtype=jnp.int32)
    kpos_s = jnp.arange(past_len + L, dtype=jnp.int32)
    bias_s = _bias_of_dist(qpos_s[:, None] - kpos_s[None, :], rel_bias)
    bias_s = jnp.broadcast_to(bias_s[:, None], (n_heads, 2, L, past_len + L)).reshape(2 * n_heads * L, -1)
    bias_past = bias_s[:, :past_len].reshape(bias_s.shape[0], -1, DECODE_PAGES * page).transpose(1, 0, 2)
    bias_new = jnp.pad(bias_s[:, past_len:], ((0, 0), (0, LANES - L)), constant_values=NEG_INF)

    xp = x_prompt
    xs = x_sample.reshape(DB * L, D)
    outs = [[] for _ in range(6)]
    for layer in range(depth):
        lam_init = _lambda_init(layer)
        w_in_b = w_in[layer].astype(BF16)
        w_out_b = w_out[layer].astype(BF16)
        pool_w_b = pool_w[layer].astype(BF16)
        ps = pool_scale[layer][None]
        lams = [v[layer][None] for v in (lambda_q1, lambda_k1, lambda_q2, lambda_k2)]
        g, lg, lb = subln_g[layer][None], ln_g[layer][None], ln_b[layer][None]
        ck = cache_k[layer].reshape(n_pool_pages, page * n_heads, V_DIM)
        cv = cache_v[layer].reshape(n_pool_pages, page * n_heads, V_DIM)

        k_p, v_p, q_p, kb_p, vb_p, gp_p, sz_p, pp = _proj_prompt(xp, w_in_b, pool_w_b, ps, d_pool, d_attn, n_heads)
        yp = _attn_prompt(q_p, kb_p, vb_p, bias_tab, lams, gp_p, sz_p, xp, w_out_b, g, lg, lb,
                          lam_init, alpha, n_heads)

        k_s, v_s, q_s, gp_s, sz_s, ps_new = _proj_sample(xs, state_pool[layer], w_in_b, pool_w_b, ps, L,
                                                        past_len, d_pool, d_attn, n_heads)
        o_s = _attn_decode(page_table, q_s.reshape(DB, L, d_attn),
                           k_s.reshape(DB, L, d_attn), v_s.reshape(DB, L, d_attn),
                           ck, cv, bias_past, bias_new, lams, lam_init, n_heads)
        ys = _finish_sample(o_s.reshape(DB * L, d_attn), sz_s, gp_s, xs, w_out_b, g, lg, lb,
                            lam_init, alpha, n_heads)

        outs[0].append(k_p.reshape(B, S, n_heads, V_DIM))
        outs[1].append(v_p.reshape(B, S, n_heads, V_DIM))
        outs[2].append(pp[:, HALO - POOL_BUF:])
        outs[3].append(k_s.reshape(DB, L, n_heads, V_DIM))
        outs[4].append(v_s.reshape(DB, L, n_heads, V_DIM))
        outs[5].append(ps_new)
        xp, xs = yp, ys

    return (xp, xs.reshape(DB, L, D), *(jnp.stack(o) for o in outs))
```

```python
import functools
import math

import jax
import jax.numpy as jnp
from jax import lax
from jax.experimental import pallas as pl
from jax.experimental.pallas import tpu as pltpu

F32 = jnp.float32
BF16 = jnp.bfloat16

POOL_WINDOWS = (2, 4, 8, 16)
POOL_BUF = max(POOL_WINDOWS) - 1
HEAD_DIM = 64
V_DIM = 2 * HEAD_DIM
ATTN_SCALE = HEAD_DIM ** -0.5
NUM_BUCKETS = 32
MAX_DISTANCE = 128
MAX_EXACT = NUM_BUCKETS // 2
NEG_INF = -1e30
LN_EPS = 1e-5
SUBLN_EPS = 1e-5

LANES = 128
SUBLANES = 8
HALO = 2 * SUBLANES
VMEM_LIMIT_BYTES = 56 * 1024 * 1024

PROJ_ROWS = 512
ATTN_TILE = 256
DECODE_PAGES = 16
NT_DIMS = (((1,), (1,)), ((), ()))


def _lambda_init(layer):
    return 0.8 - 0.6 * math.exp(-0.3 * layer)


def _silu(z):
    return z / (1.0 + jnp.exp(-z))


def _lam_value(lq1, lk1, lq2, lk2, lam_init):
    a = jnp.sum(lq1 * lk1, axis=-1, keepdims=True)
    b = jnp.sum(lq2 * lk2, axis=-1, keepdims=True)
    return jnp.exp(a) - jnp.exp(b) + lam_init


def _compiler_params(n_axes):
    return pltpu.CompilerParams(dimension_semantics=("arbitrary",) * n_axes,
                                vmem_limit_bytes=VMEM_LIMIT_BYTES)


def _pool_branch(u, window_sum, cnt_of, pw_ref, ps_ref):
    outs = []
    for g, w in enumerate(POOL_WINDOWS):
        sl = slice(g * LANES, (g + 1) * LANES)
        d = window_sum(g, w) / cnt_of(w) - u[:, sl]
        outs.append(jnp.dot(d.astype(BF16), pw_ref[g], preferred_element_type=F32))
    return jnp.concatenate(outs, axis=-1) * ps_ref[...]


def _store_heads_interleaved(dst_ref, val, rows, n_heads):
    for h in range(n_heads):
        dst_ref[pl.ds(h, rows, stride=n_heads), :] = val[:, h * V_DIM:(h + 1) * V_DIM]


def _proj_prompt_kernel(x_ref, w_ref, pw_ref, ps_ref,
                        k_ref, v_ref, q_ref, kb_ref, vt_ref, gp_ref, sz_ref, pp_ref,
                        ext_ref, *, bm, d_pool, d_attn, n_heads):
    s = pl.program_id(1)
    xb = x_ref[...].astype(BF16)
    offs = [0, d_pool, 2 * d_pool, 2 * d_pool + d_attn, 2 * d_pool + 2 * d_attn,
            2 * d_pool + 3 * d_attn, 2 * d_pool + 4 * d_attn]

    def proj(j):
        return jnp.dot(xb, w_ref[:, offs[j]:offs[j + 1]], preferred_element_type=F32)

    pu = proj(0)

    @pl.when(s == 0)
    def _():
        ext_ref[0:HALO, :] = jnp.zeros((HALO, d_pool), F32)

    ext_ref[HALO:HALO + bm, :] = pu
    pos = s * bm + lax.broadcasted_iota(jnp.int32, (bm, 1), 0)

    def window_sum(g, w):
        sl = slice(g * LANES, (g + 1) * LANES)
        acc = pu[:, sl]
        for sh in range(1, w):
            acc = acc + ext_ref[HALO - sh:HALO - sh + bm, sl]
        return acc

    def cnt_of(w):
        return jnp.minimum(pos + 1, w).astype(F32)

    pool_y = _pool_branch(pu, window_sum, cnt_of, pw_ref, ps_ref)
    ext_ref[0:HALO, :] = pu[bm - HALO:, :]

    gp_ref[...] = (pool_y * _silu(proj(1))).astype(BF16)
    q_ref[...] = (proj(2) * ATTN_SCALE).astype(BF16)
    k = proj(3)
    kb_ref[...] = k.astype(BF16)
    _store_heads_interleaved(k_ref, k, bm, n_heads)
    v = proj(4)
    for t in range(bm // ATTN_TILE):
        vt_ref[t] = v[t * ATTN_TILE:(t + 1) * ATTN_TILE, :].T.astype(BF16)
    _store_heads_interleaved(v_ref, v, bm, n_heads)
    sz_ref[...] = _silu(proj(5))

    @pl.when(s == pl.num_programs(1) - 1)
    def _():
        pp_ref[...] = pu[bm - HALO:, :]


def _proj_prompt(x, w_in, pool_w, pool_scale, d_pool, d_attn, n_heads):
    B, S, D = x.shape
    bm = PROJ_ROWS
    T = ATTN_TILE
    assert S % bm == 0 and bm >= HALO and bm % T == 0
    d_in = w_in.shape[1]
    row_blk = lambda width: pl.BlockSpec((None, bm, width), lambda b, s: (b, s, 0))
    const2 = lambda shape: pl.BlockSpec(shape, lambda b, s: (0,) * len(shape))
    out_shape = (
        jax.ShapeDtypeStruct((B, S * n_heads, V_DIM), F32),
        jax.ShapeDtypeStruct((B, S * n_heads, V_DIM), F32),
        jax.ShapeDtypeStruct((B, S, d_attn), BF16),
        jax.ShapeDtypeStruct((B, S, d_attn), BF16),
        jax.ShapeDtypeStruct((B, S // T, d_attn, T), BF16),
        jax.ShapeDtypeStruct((B, S, d_pool), BF16),
        jax.ShapeDtypeStruct((B, S, d_attn), F32),
        jax.ShapeDtypeStruct((B, HALO, d_pool), F32),
    )
    kv_blk = pl.BlockSpec((None, bm * n_heads, V_DIM), lambda b, s: (b, s, 0))
    vt_blk = pl.BlockSpec((None, bm // T, d_attn, T), lambda b, s: (b, s, 0, 0))
    out_specs = (kv_blk, kv_blk, row_blk(d_attn), row_blk(d_attn), vt_blk,
                 row_blk(d_pool), row_blk(d_attn),
                 pl.BlockSpec((None, HALO, d_pool), lambda b, s: (b, 0, 0)))
    return pl.pallas_call(
        functools.partial(_proj_prompt_kernel, bm=bm, d_pool=d_pool, d_attn=d_attn, n_heads=n_heads),
        grid=(B, S // bm),
        in_specs=[row_blk(D), const2((D, d_in)), const2(pool_w.shape), const2(pool_scale.shape)],
        out_specs=out_specs,
        out_shape=out_shape,
        scratch_shapes=[pltpu.VMEM((HALO + bm, d_pool), F32)],
        compiler_params=_compiler_params(2),
        name="proj_prompt",
    )(x, w_in, pool_w, pool_scale)


def _proj_sample_kernel(x_ref, st_ref, w_ref, pw_ref, ps_ref,
                        k_ref, v_ref, q_ref, gp_ref, sz_ref, ps_out_ref,
                        ext_ref, *, nb, L, pos0, d_pool, d_attn, n_heads):
    rows = nb * L
    xb = x_ref[...].astype(BF16)
    offs = [0, d_pool, 2 * d_pool, 2 * d_pool + d_attn, 2 * d_pool + 2 * d_attn,
            2 * d_pool + 3 * d_attn, 2 * d_pool + 4 * d_attn]

    def proj(j):
        return jnp.dot(xb, w_ref[:, offs[j]:offs[j + 1]], preferred_element_type=F32)

    su = proj(0)
    ext_ref[:, HALO - POOL_BUF:HALO, :] = st_ref[...]
    ext_ref[:, HALO:HALO + L, :] = su.reshape(nb, L, d_pool)
    pos = pos0 + lax.rem(lax.broadcasted_iota(jnp.int32, (rows, 1), 0), L)

    def window_sum(g, w):
        sl = slice(g * LANES, (g + 1) * LANES)
        acc = ext_ref[:, HALO:HALO + L, sl]
        for sh in range(1, w):
            acc = acc + ext_ref[:, HALO - sh:HALO - sh + L, sl]
        return acc.reshape(rows, LANES)

    def cnt_of(w):
        return jnp.minimum(pos + 1, w).astype(F32)

    pool_y = _pool_branch(su, window_sum, cnt_of, pw_ref, ps_ref)
    ps_out_ref[...] = ext_ref[:, HALO + L - POOL_BUF:HALO + L, :]

    gp_ref[...] = (pool_y * _silu(proj(1))).astype(BF16)
    q_ref[...] = proj(2) * ATTN_SCALE
    _store_heads_interleaved(k_ref, proj(3), rows, n_heads)
    _store_heads_interleaved(v_ref, proj(4), rows, n_heads)
    sz_ref[...] = _silu(proj(5))


def _proj_sample(x2d, state, w_in, pool_w, pool_scale, L, pos0, d_pool, d_attn, n_heads):
    rows_total, D = x2d.shape
    DB = state.shape[0]
    assert L == SUBLANES and rows_total == DB * L
    rows = min(PROJ_ROWS, rows_total)
    nb = rows // L
    assert rows_total % rows == 0
    d_in = w_in.shape[1]
    row_blk = lambda width: pl.BlockSpec((rows, width), lambda i: (i, 0))
    const = lambda shape: pl.BlockSpec(shape, lambda i: (0,) * len(shape))
    st_blk = pl.BlockSpec((nb, POOL_BUF, d_pool), lambda i: (i, 0, 0))
    kv_blk = pl.BlockSpec((rows * n_heads, V_DIM), lambda i: (i, 0))
    out_shape = (
        jax.ShapeDtypeStruct((rows_total * n_heads, V_DIM), F32),
        jax.ShapeDtypeStruct((rows_total * n_heads, V_DIM), F32),
        jax.ShapeDtypeStruct((rows_total, d_attn), F32),
        jax.ShapeDtypeStruct((rows_total, d_pool), BF16),
        jax.ShapeDtypeStruct((rows_total, d_attn), F32),
        jax.ShapeDtypeStruct((DB, POOL_BUF, d_pool), F32),
    )
    return pl.pallas_call(
        functools.partial(_proj_sample_kernel, nb=nb, L=L, pos0=pos0, d_pool=d_pool, d_attn=d_attn,
                          n_heads=n_heads),
        grid=(rows_total // rows,),
        in_specs=[row_blk(D), st_blk, const((D, d_in)), const(pool_w.shape), const(pool_scale.shape)],
        out_specs=(kv_blk, kv_blk, row_blk(d_attn), row_blk(d_pool), row_blk(d_attn), st_blk),
        out_shape=out_shape,
        scratch_shapes=[pltpu.VMEM((nb, HALO + L, d_pool), F32)],
        compiler_params=_compiler_params(1),
        name="proj_sample",
    )(x2d, state, w_in, pool_w, pool_scale)


def _finish_rows(normed_heads, sz, gp, x, wout_ref, lng_ref, lnb_ref, alpha):
    o = jnp.concatenate(normed_heads, axis=-1) * sz
    gated = jnp.concatenate([gp, o.astype(BF16)], axis=-1)
    h = jnp.dot(gated, wout_ref[...], preferred_element_type=F32)
    r = alpha * x + h
    mu = jnp.mean(r, axis=-1, keepdims=True)
    rc = r - mu
    var = jnp.mean(rc * rc, axis=-1, keepdims=True)
    return rc * lax.rsqrt(var + LN_EPS) * lng_ref[...] + lnb_ref[...]


def _attn_prompt_kernel(q_ref, kb_ref, vt_ref, bias_ref, lq1_ref, lk1_ref, lq2_ref, lk2_ref,
                        gp_ref, sz_ref, x_ref, wout_ref, g_ref, lng_ref, lnb_ref,
                        y_ref, m_ref, l_ref, acc_ref, s0_ref, *, T, n_heads, lam_init, alpha):
    qi = pl.program_id(1)
    q = q_ref[...].astype(F32)
    lane = lax.broadcasted_iota(jnp.int32, (T, V_DIM), 1)
    qq = []
    for h in range(n_heads):
        qh = q[:, h * V_DIM:(h + 1) * V_DIM]
        qq.append(jnp.concatenate([jnp.where(lane < HEAD_DIM, qh, 0.0),
                                   jnp.where(lane >= HEAD_DIM, qh, 0.0)], axis=0).astype(BF16))

    m_ref[...] = jnp.full(m_ref.shape, -jnp.inf, F32)
    l_ref[...] = jnp.zeros(l_ref.shape, F32)
    acc_ref[...] = jnp.zeros(acc_ref.shape, F32)

    def scores(j, h):
        kh = kb_ref[pl.ds(pl.multiple_of(j * T, T), T), :][:, h * V_DIM:(h + 1) * V_DIM]
        return lax.dot_general(kh, qq[h], NT_DIMS, preferred_element_type=F32)

    def body(j, carry):
        vt = vt_ref[j]
        bidx = jnp.minimum(qi - j, 2)
        s_next = s0_ref[...]
        for h in range(n_heads):
            s12 = s_next
            s_next = scores(j, h + 1) if h + 1 < n_heads else scores(jnp.minimum(j + 1, qi), 0)
            vth = vt[h * V_DIM:(h + 1) * V_DIM, :]
            bias = bias_ref[bidx, h]
            for c in range(2):
                i = 2 * h + c
                s = s12[:, c * T:(c + 1) * T] + bias
                m_old = m_ref[i]
                m_new = jnp.maximum(m_old, jnp.max(s, axis=0, keepdims=True))
                a = jnp.exp(m_old - m_new)
                p = jnp.exp(s - m_new)
                l_ref[i] = a * l_ref[i] + jnp.sum(p, axis=0, keepdims=True)
                acc_ref[i] = a * acc_ref[i] + jnp.dot(vth, p.astype(BF16), preferred_element_type=F32)
                m_ref[i] = m_new
        s0_ref[...] = s_next
        return carry

    @pl.when(qi >= 0)
    def _():
        s0_ref[...] = scores(0, 0)

    lax.fori_loop(0, qi + 1, body, 0)

    lam = _lam_value(lq1_ref[...], lk1_ref[...], lq2_ref[...], lk2_ref[...], lam_init)
    normed = []
    for h in range(n_heads):
        ot = acc_ref[2 * h] / l_ref[2 * h] - lam * (acc_ref[2 * h + 1] / l_ref[2 * h + 1])
        ms = jnp.mean(ot * ot, axis=0, keepdims=True)
        normed.append((ot * lax.rsqrt(ms + SUBLN_EPS)).T * g_ref[...] * (1.0 - lam_init))
    y_ref[...] = _finish_rows(normed, sz_ref[...], gp_ref[...], x_ref[...], wout_ref, lng_ref, lnb_ref, alpha)


def _attn_prompt(q, kb, vt, bias_tab, lams, gp, sz, x, w_out, subln_g, ln_g, ln_b, lam_init, alpha, n_heads):
    B, S, D = x.shape
    T = ATTN_TILE
    assert S % T == 0 and T >= MAX_DISTANCE
    d_attn = q.shape[-1]
    row_blk = lambda width: pl.BlockSpec((None, T, width), lambda b, i: (b, i, 0))
    seq_blk = pl.BlockSpec((None, S, d_attn), lambda b, i: (b, 0, 0))
    vt_blk = pl.BlockSpec((None, S // T, d_attn, T), lambda b, i: (b, 0, 0, 0))
    const = lambda a: pl.BlockSpec(a.shape, lambda b, i: (0,) * a.ndim)
    return pl.pallas_call(
        functools.partial(_attn_prompt_kernel, T=T, n_heads=n_heads, lam_init=lam_init, alpha=alpha),
        grid=(B, S // T),
        in_specs=[row_blk(d_attn), seq_blk, vt_blk, const(bias_tab)] + [const(l) for l in lams]
                 + [row_blk(gp.shape[-1]), row_blk(d_attn), row_blk(D), const(w_out), const(subln_g),
                    const(ln_g), const(ln_b)],
        out_specs=row_blk(D),
        out_shape=jax.ShapeDtypeStruct((B, S, D), F32),
        scratch_shapes=[pltpu.VMEM((2 * n_heads, 1, T), F32), pltpu.VMEM((2 * n_heads, 1, T), F32),
                        pltpu.VMEM((2 * n_heads, V_DIM, T), F32), pltpu.VMEM((T, 2 * T), F32)],
        compiler_params=_compiler_params(2),
        name="attn_prompt",
    )(q, kb, vt, bias_tab, *lams, gp, sz, x, w_out, subln_g, ln_g, ln_b)


def _deinterleave_heads(page_ref, rows, n_heads):
    return jnp.concatenate(
        [page_ref[pl.ds(h, rows, stride=n_heads), :].astype(BF16) for h in range(n_heads)], axis=-1)


def _attn_decode_kernel(pt_ref, q_ref, sk_ref, sv_ref, bias_ref, biasn_ref,
                        lq1_ref, lk1_ref, lq2_ref, lk2_ref, ck_hbm, cv_hbm,
                        o_ref, kbuf, vbuf, sem, qbd_ref, m_ref, l_ref, acc_ref,
                        *, P, page, L, n_heads, n_chunks, n_batch, lam_init):
    d_attn = n_heads * V_DIM
    n_steps = n_batch * n_chunks
    t = pl.program_id(0)
    slot = lax.rem(t, 2)
    c = lax.rem(t, n_chunks)

    def page_copies(t, slot):
        copies = []
        for i in range(P):
            pg = pt_ref[t * P + i]
            copies.append(pltpu.make_async_copy(ck_hbm.at[pg], kbuf.at[slot, i], sem.at[0, slot]))
            copies.append(pltpu.make_async_copy(cv_hbm.at[pg], vbuf.at[slot, i], sem.at[1, slot]))
        return copies

    def update(k_all, v_all, bias):
        s = lax.dot_general(qbd_ref[...], k_all, NT_DIMS, preferred_element_type=F32) + bias
        m_old = m_ref[...]
        m_new = jnp.maximum(m_old, jnp.max(s, axis=-1, keepdims=True))
        a = jnp.exp(m_old - m_new)
        p = jnp.exp(s - m_new)
        l_ref[...] = a * l_ref[...] + jnp.sum(p, axis=-1, keepdims=True)
        acc_ref[...] = a * acc_ref[...] + jnp.dot(p.astype(BF16), v_all, preferred_element_type=F32)
        m_ref[...] = m_new

    @pl.when(t == 0)
    def _():
        for cp in page_copies(0, 0):
            cp.start()

    @pl.when(t + 1 < n_steps)
    def _():
        for cp in page_copies(t + 1, 1 - slot):
            cp.start()

    for cp in page_copies(t, slot):
        cp.wait()

    @pl.when(c == 0)
    def _():
        q = q_ref[...]
        lane = lax.broadcasted_iota(jnp.int32, (L, d_attn), 1)
        rows = []
        for h in range(n_heads):
            for cc in range(2):
                lo = h * V_DIM + cc * HEAD_DIM
                rows.append(jnp.where((lane >= lo) & (lane < lo + HEAD_DIM), q, 0.0))
        qbd_ref[...] = jnp.concatenate(rows, axis=0).astype(BF16)
        m_ref[...] = jnp.full(m_ref.shape, -jnp.inf, F32)
        l_ref[...] = jnp.zeros(l_ref.shape, F32)
        acc_ref[...] = jnp.zeros(acc_ref.shape, F32)

    k_all = jnp.concatenate([_deinterleave_heads(kbuf.at[slot, i], page, n_heads) for i in range(P)], axis=0)
    v_all = jnp.concatenate([_deinterleave_heads(vbuf.at[slot, i], page, n_heads) for i in range(P)], axis=0)
    update(k_all, v_all, bias_ref[c])

    @pl.when(c == n_chunks - 1)
    def _():
        pad = jnp.zeros((LANES - L, d_attn), BF16)
        update(jnp.concatenate([sk_ref[...].astype(BF16), pad], axis=0),
               jnp.concatenate([sv_ref[...].astype(BF16), pad], axis=0), biasn_ref[...])
        lam = _lam_value(lq1_ref[...], lk1_ref[...], lq2_ref[...], lk2_ref[...], lam_init)
        out = acc_ref[...] / l_ref[...]
        heads = []
        for h in range(n_heads):
            r0 = 2 * h * L
            cols = slice(h * V_DIM, (h + 1) * V_DIM)
            heads.append(out[r0:r0 + L, cols] - lam * out[r0 + L:r0 + 2 * L, cols])
        o_ref[...] = jnp.concatenate(heads, axis=-1)


def _attn_decode(page_table, q, sk, sv, cache_k, cache_v, bias_past, bias_new, lams, lam_init, n_heads):
    DB, L, d_attn = q.shape
    n_pages = page_table.shape[1]
    rows_per_page = cache_k.shape[1]
    page = rows_per_page // n_heads
    P = DECODE_PAGES
    assert n_pages % P == 0 and L == SUBLANES
    n_rows = 2 * n_heads * L
    n_chunks = n_pages // P
    whole = lambda a: pl.BlockSpec(a.shape, lambda t: (0,) * a.ndim)
    tok_blk = pl.BlockSpec((None, L, d_attn), lambda t: (t // n_chunks, 0, 0))
    in_specs = ([pl.BlockSpec(memory_space=pltpu.SMEM), tok_blk, tok_blk, tok_blk,
                 whole(bias_past), whole(bias_new)] + [whole(l) for l in lams]
                + [pl.BlockSpec(memory_space=pl.ANY)] * 2)
    return pl.pallas_call(
        functools.partial(_attn_decode_kernel, P=P, page=page, L=L, n_heads=n_heads,
                          n_chunks=n_chunks, n_batch=DB, lam_init=lam_init),
        grid=(DB * n_chunks,),
        in_specs=in_specs,
        out_specs=tok_blk,
        out_shape=jax.ShapeDtypeStruct((DB, L, d_attn), F32),
        scratch_shapes=[pltpu.VMEM((2, P, rows_per_page, V_DIM), F32),
                        pltpu.VMEM((2, P, rows_per_page, V_DIM), F32),
                        pltpu.SemaphoreType.DMA((2, 2)),
                        pltpu.VMEM((n_rows, d_attn), BF16), pltpu.VMEM((n_rows, 1), F32),
                        pltpu.VMEM((n_rows, 1), F32), pltpu.VMEM((n_rows, d_attn), F32)],
        compiler_params=_compiler_params(1),
        name="attn_decode",
    )(page_table.reshape(-1), q, sk, sv, bias_past, bias_new, *lams, cache_k, cache_v)


def _finish_sample_kernel(o_ref, sz_ref, gp_ref, x_ref, wout_ref, g_ref, lng_ref, lnb_ref, y_ref,
                          *, n_heads, lam_init, alpha):
    o = o_ref[...]
    normed = []
    for h in range(n_heads):
        oh = o[:, h * V_DIM:(h + 1) * V_DIM]
        ms = jnp.mean(oh * oh, axis=-1, keepdims=True)
        normed.append(oh * lax.rsqrt(ms + SUBLN_EPS) * g_ref[...] * (1.0 - lam_init))
    y_ref[...] = _finish_rows(normed, sz_ref[...], gp_ref[...], x_ref[...], wout_ref, lng_ref, lnb_ref, alpha)


def _finish_sample(o, sz, gp, x2d, w_out, subln_g, ln_g, ln_b, lam_init, alpha, n_heads):
    rows_total, D = x2d.shape
    rows = min(PROJ_ROWS, rows_total)
    assert rows_total % rows == 0
    row_blk = lambda a: pl.BlockSpec((rows, a.shape[-1]), lambda i: (i, 0))
    const = lambda a: pl.BlockSpec(a.shape, lambda i: (0,) * a.ndim)
    return pl.pallas_call(
        functools.partial(_finish_sample_kernel, n_heads=n_heads, lam_init=lam_init, alpha=alpha),
        grid=(rows_total // rows,),
        in_specs=[row_blk(o), row_blk(sz), row_blk(gp), row_blk(x2d), const(w_out), const(subln_g),
                  const(ln_g), const(ln_b)],
        out_specs=row_blk(x2d),
        out_shape=jax.ShapeDtypeStruct((rows_total, D), F32),
        compiler_params=_compiler_params(1),
        name="finish_sample",
    )(o, sz, gp, x2d, w_out, subln_g, ln_g, ln_b)


def _bucket(dist):
    is_small = dist < MAX_EXACT
    d = jnp.maximum(dist, 1).astype(F32)
    large = MAX_EXACT + (jnp.log(d / MAX_EXACT) / math.log(MAX_DISTANCE / MAX_EXACT)
                         * (NUM_BUCKETS - MAX_EXACT)).astype(jnp.int32)
    large = jnp.minimum(large, NUM_BUCKETS - 1)
    return jnp.where(is_small, dist, large)


def _bias_of_dist(dist, table):
    bucket = _bucket(jnp.maximum(dist, 0))[None]
    t = table.astype(F32)
    b = jnp.zeros((t.shape[1],) + dist.shape, F32)
    for i in range(NUM_BUCKETS):
        b = jnp.where(bucket == i, t[i][:, None, None], b)
    return jnp.where((dist >= 0)[None], b, NEG_INF)


def _prompt_bias_tiles(table, T):
    i = jnp.arange(T, dtype=jnp.int32)
    base = i[None, :] - i[:, None]
    return jnp.stack([_bias_of_dist(base + off * T, table) for off in range(3)])


def kernel(x_prompt, x_sample, cache_k, cache_v, state_pool, page_table, w_in, pool_w, pool_scale,
           lambda_q1, lambda_k1, lambda_q2, lambda_k2, subln_g, rel_bias, w_out, ln_g, ln_b):
    B, S, D = x_prompt.shape
    DB, L, _ = x_sample.shape
    depth = w_in.shape[0]
    n_pool_pages, page, n_heads, hd2 = cache_k.shape[1:]
    assert hd2 == V_DIM
    n_pages = page_table.shape[1]
    past_len = n_pages * page
    d_attn = n_heads * V_DIM
    d_pool = pool_scale.shape[-1]
    assert d_pool == len(POOL_WINDOWS) * LANES and w_in.shape[-1] == 2 * d_pool + 4 * d_attn
    alpha = (2 * depth) ** 0.25

    bias_tab = _prompt_bias_tiles(rel_bias, ATTN_TILE)
    qpos_s = past_len + jnp.arange(L, dtype=jnp.int32)
    kpos_s = jnp.arange(past_len + L, dtype=jnp.int32)
    bias_s = _bias_of_dist(qpos_s[:, None] - kpos_s[None, :], rel_bias)
    bias_s = jnp.broadcast_to(bias_s[:, None], (n_heads, 2, L, past_len + L)).reshape(2 * n_heads * L, -1)
    bias_past = bias_s[:, :past_len].reshape(bias_s.shape[0], -1, DECODE_PAGES * page).transpose(1, 0, 2)
    bias_new = jnp.pad(bias_s[:, past_len:], ((0, 0), (0, LANES - L)), constant_values=NEG_INF)

    xp = x_prompt
    xs = x_sample.reshape(DB * L, D)
    outs = [[] for _ in range(6)]
    for layer in range(depth):
        lam_init = _lambda_init(layer)
        w_in_b = w_in[layer].astype(BF16)
        w_out_b = w_out[layer].astype(BF16)
        pool_w_b = pool_w[layer].astype(BF16)
        ps = pool_scale[layer][None]
        lams = [v[layer][None] for v in (lambda_q1, lambda_k1, lambda_q2, lambda_k2)]
        g, lg, lb = subln_g[layer][None], ln_g[layer][None], ln_b[layer][None]
        ck = cache_k[layer].reshape(n_pool_pages, page * n_heads, V_DIM)
        cv = cache_v[layer].reshape(n_pool_pages, page * n_heads, V_DIM)

        k_p, v_p, q_p, kb_p, vt_p, gp_p, sz_p, pp = _proj_prompt(xp, w_in_b, pool_w_b, ps, d_pool, d_attn, n_heads)
        yp = _attn_prompt(q_p, kb_p, vt_p, bias_tab, lams, gp_p, sz_p, xp, w_out_b, g, lg, lb,
                          lam_init, alpha, n_heads)

        k_s, v_s, q_s, gp_s, sz_s, ps_new = _proj_sample(xs, state_pool[layer], w_in_b, pool_w_b, ps, L,
                                                        past_len, d_pool, d_attn, n_heads)
        o_s = _attn_decode(page_table, q_s.reshape(DB, L, d_attn),
                           k_s.reshape(DB, L, d_attn), v_s.reshape(DB, L, d_attn),
                           ck, cv, bias_past, bias_new, lams, lam_init, n_heads)
        ys = _finish_sample(o_s.reshape(DB * L, d_attn), sz_s, gp_s, xs, w_out_b, g, lg, lb,
                            lam_init, alpha, n_heads)

        outs[0].append(k_p.reshape(B, S, n_heads, V_DIM))
        outs[1].append(v_p.reshape(B, S, n_heads, V_DIM))
        outs[2].append(pp[:, HALO - POOL_BUF:])
        outs[3].append(k_s.reshape(DB, L, n_heads, V_DIM))
        outs[4].append(v_s.reshape(DB, L, n_heads, V_DIM))
        outs[5].append(ps_new)
        xp, xs = yp, ys

    return (xp, xs.reshape(DB, L, D), *(jnp.stack(o) for o in outs))
```

```python
import functools
import math

import jax
import jax.numpy as jnp
from jax import lax
from jax.experimental import pallas as pl
from jax.experimental.pallas import tpu as pltpu

F32 = jnp.float32
BF16 = jnp.bfloat16

POOL_WINDOWS = (2, 4, 8, 16)
POOL_BUF = max(POOL_WINDOWS) - 1
HEAD_DIM = 64
V_DIM = 2 * HEAD_DIM
ATTN_SCALE = HEAD_DIM ** -0.5
LOG2E = math.log2(math.e)
NUM_BUCKETS = 32
MAX_DISTANCE = 128
MAX_EXACT = NUM_BUCKETS // 2
NEG_INF = -1e30
LN_EPS = 1e-5
SUBLN_EPS = 1e-5

LANES = 128
SUBLANES = 8
HALO = 2 * SUBLANES
VMEM_LIMIT_BYTES = 56 * 1024 * 1024

PROJ_ROWS = 512
ATTN_TILE = 256
DECODE_PAGES = 16
NT_DIMS = (((1,), (1,)), ((), ()))


def _lambda_init(layer):
    return 0.8 - 0.6 * math.exp(-0.3 * layer)


def _silu(z):
    return z / (1.0 + jnp.exp(-z))


def _lam_value(lq1, lk1, lq2, lk2, lam_init):
    a = jnp.sum(lq1 * lk1, axis=-1, keepdims=True)
    b = jnp.sum(lq2 * lk2, axis=-1, keepdims=True)
    return jnp.exp(a) - jnp.exp(b) + lam_init


def _compiler_params(n_axes):
    return pltpu.CompilerParams(dimension_semantics=("arbitrary",) * n_axes,
                                vmem_limit_bytes=VMEM_LIMIT_BYTES)


def _pool_diffs(u, window_sum, cnt_of):
    diffs = []
    for g, w in enumerate(POOL_WINDOWS):
        sl = slice(g * LANES, (g + 1) * LANES)
        diffs.append((window_sum(g, w) / cnt_of(w) - u[:, sl]).astype(BF16))
    return diffs


def _pool_mix(diffs, pw_ref, ps_ref):
    outs = [jnp.dot(d, pw_ref[g], preferred_element_type=F32) for g, d in enumerate(diffs)]
    return jnp.concatenate(outs, axis=-1) * ps_ref[...]


def _store_heads_interleaved(dst_ref, val, rows, n_heads):
    for h in range(n_heads):
        dst_ref[pl.ds(h, rows, stride=n_heads), :] = val[:, h * V_DIM:(h + 1) * V_DIM]


def _proj_prompt_kernel(x_ref, w_ref, pw_ref, ps_ref,
                        k_ref, v_ref, q_ref, kb_ref, vt_ref, gp_ref, sz_ref, pp_ref,
                        ext_ref, *, bm, d_pool, d_attn, n_heads):
    s = pl.program_id(1)
    xb = x_ref[...].astype(BF16)
    offs = [0, d_pool, 2 * d_pool, 2 * d_pool + d_attn, 2 * d_pool + 2 * d_attn,
            2 * d_pool + 3 * d_attn, 2 * d_pool + 4 * d_attn]

    def proj(j):
        return jnp.dot(xb, w_ref[:, offs[j]:offs[j + 1]], preferred_element_type=F32)

    pu = proj(0)

    @pl.when(s == 0)
    def _():
        ext_ref[0:HALO, :] = jnp.zeros((HALO, d_pool), F32)

    ext_ref[HALO:HALO + bm, :] = pu
    pos = s * bm + lax.broadcasted_iota(jnp.int32, (bm, 1), 0)

    def window_sum(g, w):
        sl = slice(g * LANES, (g + 1) * LANES)
        acc = pu[:, sl]
        for sh in range(1, w):
            acc = acc + ext_ref[HALO - sh:HALO - sh + bm, sl]
        return acc

    def cnt_of(w):
        return jnp.minimum(pos + 1, w).astype(F32)

    diffs = _pool_diffs(pu, window_sum, cnt_of)
    ext_ref[0:HALO, :] = pu[bm - HALO:, :]
    gate = _silu(proj(1))
    q_ref[...] = (proj(2) * (ATTN_SCALE * LOG2E)).astype(BF16)
    k = proj(3)
    kb_ref[...] = k.astype(BF16)
    _store_heads_interleaved(k_ref, k, bm, n_heads)
    gp_ref[...] = (_pool_mix(diffs, pw_ref, ps_ref) * gate).astype(BF16)
    v = proj(4)
    for t in range(bm // ATTN_TILE):
        vt_ref[t] = v[t * ATTN_TILE:(t + 1) * ATTN_TILE, :].T.astype(BF16)
    _store_heads_interleaved(v_ref, v, bm, n_heads)
    sz_ref[...] = _silu(proj(5))

    @pl.when(s == pl.num_programs(1) - 1)
    def _():
        pp_ref[...] = pu[bm - HALO:, :]


def _proj_prompt(x, w_in, pool_w, pool_scale, d_pool, d_attn, n_heads):
    B, S, D = x.shape
    bm = PROJ_ROWS
    T = ATTN_TILE
    assert S % bm == 0 and bm >= HALO and bm % T == 0
    d_in = w_in.shape[1]
    row_blk = lambda width: pl.BlockSpec((None, bm, width), lambda b, s: (b, s, 0))
    const2 = lambda shape: pl.BlockSpec(shape, lambda b, s: (0,) * len(shape))
    out_shape = (
        jax.ShapeDtypeStruct((B, S * n_heads, V_DIM), F32),
        jax.ShapeDtypeStruct((B, S * n_heads, V_DIM), F32),
        jax.ShapeDtypeStruct((B, S, d_attn), BF16),
        jax.ShapeDtypeStruct((B, S, d_attn), BF16),
        jax.ShapeDtypeStruct((B, S // T, d_attn, T), BF16),
        jax.ShapeDtypeStruct((B, S, d_pool), BF16),
        jax.ShapeDtypeStruct((B, S, d_attn), F32),
        jax.ShapeDtypeStruct((B, HALO, d_pool), F32),
    )
    kv_blk = pl.BlockSpec((None, bm * n_heads, V_DIM), lambda b, s: (b, s, 0))
    vt_blk = pl.BlockSpec((None, bm // T, d_attn, T), lambda b, s: (b, s, 0, 0))
    out_specs = (kv_blk, kv_blk, row_blk(d_attn), row_blk(d_attn), vt_blk,
                 row_blk(d_pool), row_blk(d_attn),
                 pl.BlockSpec((None, HALO, d_pool), lambda b, s: (b, 0, 0)))
    return pl.pallas_call(
        functools.partial(_proj_prompt_kernel, bm=bm, d_pool=d_pool, d_attn=d_attn, n_heads=n_heads),
        grid=(B, S // bm),
        in_specs=[row_blk(D), const2((D, d_in)), const2(pool_w.shape), const2(pool_scale.shape)],
        out_specs=out_specs,
        out_shape=out_shape,
        scratch_shapes=[pltpu.VMEM((HALO + bm, d_pool), F32)],
        compiler_params=_compiler_params(2),
        name="proj_prompt",
    )(x, w_in, pool_w, pool_scale)


def _proj_sample_kernel(x_ref, st_ref, w_ref, pw_ref, ps_ref,
                        k_ref, v_ref, q_ref, gp_ref, sz_ref, ps_out_ref,
                        ext_ref, *, nb, L, pos0, d_pool, d_attn, n_heads):
    rows = nb * L
    xb = x_ref[...].astype(BF16)
    offs = [0, d_pool, 2 * d_pool, 2 * d_pool + d_attn, 2 * d_pool + 2 * d_attn,
            2 * d_pool + 3 * d_attn, 2 * d_pool + 4 * d_attn]

    def proj(j):
        return jnp.dot(xb, w_ref[:, offs[j]:offs[j + 1]], preferred_element_type=F32)

    su = proj(0)
    ext_ref[:, HALO - POOL_BUF:HALO, :] = st_ref[...]
    ext_ref[:, HALO:HALO + L, :] = su.reshape(nb, L, d_pool)
    pos = pos0 + lax.rem(lax.broadcasted_iota(jnp.int32, (rows, 1), 0), L)

    def window_sum(g, w):
        sl = slice(g * LANES, (g + 1) * LANES)
        acc = ext_ref[:, HALO:HALO + L, sl]
        for sh in range(1, w):
            acc = acc + ext_ref[:, HALO - sh:HALO - sh + L, sl]
        return acc.reshape(rows, LANES)

    def cnt_of(w):
        return jnp.minimum(pos + 1, w).astype(F32)

    pool_y = _pool_mix(_pool_diffs(su, window_sum, cnt_of), pw_ref, ps_ref)
    ps_out_ref[...] = ext_ref[:, HALO + L - POOL_BUF:HALO + L, :]

    gp_ref[...] = (pool_y * _silu(proj(1))).astype(BF16)
    q_ref[...] = proj(2) * ATTN_SCALE
    _store_heads_interleaved(k_ref, proj(3), rows, n_heads)
    _store_heads_interleaved(v_ref, proj(4), rows, n_heads)
    sz_ref[...] = _silu(proj(5))


def _proj_sample(x2d, state, w_in, pool_w, pool_scale, L, pos0, d_pool, d_attn, n_heads):
    rows_total, D = x2d.shape
    DB = state.shape[0]
    assert L == SUBLANES and rows_total == DB * L
    rows = min(PROJ_ROWS, rows_total)
    nb = rows // L
    assert rows_total % rows == 0
    d_in = w_in.shape[1]
    row_blk = lambda width: pl.BlockSpec((rows, width), lambda i: (i, 0))
    const = lambda shape: pl.BlockSpec(shape, lambda i: (0,) * len(shape))
    st_blk = pl.BlockSpec((nb, POOL_BUF, d_pool), lambda i: (i, 0, 0))
    kv_blk = pl.BlockSpec((rows * n_heads, V_DIM), lambda i: (i, 0))
    out_shape = (
        jax.ShapeDtypeStruct((rows_total * n_heads, V_DIM), F32),
        jax.ShapeDtypeStruct((rows_total * n_heads, V_DIM), F32),
        jax.ShapeDtypeStruct((rows_total, d_attn), F32),
        jax.ShapeDtypeStruct((rows_total, d_pool), BF16),
        jax.ShapeDtypeStruct((rows_total, d_attn), F32),
        jax.ShapeDtypeStruct((DB, POOL_BUF, d_pool), F32),
    )
    return pl.pallas_call(
        functools.partial(_proj_sample_kernel, nb=nb, L=L, pos0=pos0, d_pool=d_pool, d_attn=d_attn,
                          n_heads=n_heads),
        grid=(rows_total // rows,),
        in_specs=[row_blk(D), st_blk, const((D, d_in)), const(pool_w.shape), const(pool_scale.shape)],
        out_specs=(kv_blk, kv_blk, row_blk(d_attn), row_blk(d_pool), row_blk(d_attn), st_blk),
        out_shape=out_shape,
        scratch_shapes=[pltpu.VMEM((nb, HALO + L, d_pool), F32)],
        compiler_params=_compiler_params(1),
        name="proj_sample",
    )(x2d, state, w_in, pool_w, pool_scale)


def _finish_rows(normed_heads, sz, gp, x, wout_ref, lng_ref, lnb_ref, alpha):
    o = jnp.concatenate(normed_heads, axis=-1) * sz
    gated = jnp.concatenate([gp, o.astype(BF16)], axis=-1)
    h = jnp.dot(gated, wout_ref[...], preferred_element_type=F32)
    r = alpha * x + h
    mu = jnp.mean(r, axis=-1, keepdims=True)
    rc = r - mu
    var = jnp.mean(rc * rc, axis=-1, keepdims=True)
    return rc * lax.rsqrt(var + LN_EPS) * lng_ref[...] + lnb_ref[...]


def _attn_prompt_kernel(q_ref, kb_ref, vt_ref, bias_ref, lq1_ref, lk1_ref, lq2_ref, lk2_ref,
                        gp_ref, sz_ref, x_ref, wout_ref, g_ref, lng_ref, lnb_ref,
                        y_ref, m_ref, l_ref, acc_ref, s0_ref, *, T, n_heads, lam_init, alpha):
    qi = pl.program_id(1)
    q = q_ref[...].astype(F32)
    lane = lax.broadcasted_iota(jnp.int32, (T, V_DIM), 1)
    qq = []
    for h in range(n_heads):
        qh = q[:, h * V_DIM:(h + 1) * V_DIM]
        qq.append(jnp.concatenate([jnp.where(lane < HEAD_DIM, qh, 0.0),
                                   jnp.where(lane >= HEAD_DIM, qh, 0.0)], axis=0).astype(BF16))

    m_ref[...] = jnp.full(m_ref.shape, -jnp.inf, F32)
    l_ref[...] = jnp.zeros(l_ref.shape, F32)
    acc_ref[...] = jnp.zeros(acc_ref.shape, F32)

    def scores(j, h):
        kh = kb_ref[pl.ds(pl.multiple_of(j * T, T), T), :][:, h * V_DIM:(h + 1) * V_DIM]
        return lax.dot_general(kh, qq[h], NT_DIMS, preferred_element_type=F32)

    def kv_tile(j, near):
        vt = vt_ref[j]
        s_next = s0_ref[...]
        for h in range(n_heads):
            s12 = s_next
            s_next = scores(j, h + 1) if h + 1 < n_heads else scores(jnp.minimum(j + 1, qi), 0)
            vth = vt[h * V_DIM:(h + 1) * V_DIM, :]
            if near:
                bias = bias_ref[qi - j, h]
            for c in range(2):
                i = 2 * h + c
                s = s12[:, c * T:(c + 1) * T]
                if near:
                    s = s + bias
                m_old = m_ref[i]
                m_new = jnp.maximum(m_old, jnp.max(s, axis=0, keepdims=True))
                a = jnp.exp2(m_old - m_new)
                p = jnp.exp2(s - m_new)
                l_ref[i] = a * l_ref[i] + jnp.sum(p, axis=0, keepdims=True)
                acc_ref[i] = a * acc_ref[i] + jnp.dot(vth, p.astype(BF16), preferred_element_type=F32)
                m_ref[i] = m_new
        s0_ref[...] = s_next

    def far_body(j, carry):
        kv_tile(j, near=False)
        return carry

    def near_body(j, carry):
        kv_tile(j, near=True)
        return carry

    @pl.when(qi >= 0)
    def _():
        s0_ref[...] = scores(0, 0)

    n_far = jnp.maximum(qi - 1, 0)
    lax.fori_loop(0, n_far, far_body, 0)
    lax.fori_loop(n_far, qi + 1, near_body, 0)

    lam = _lam_value(lq1_ref[...], lk1_ref[...], lq2_ref[...], lk2_ref[...], lam_init)
    normed = []
    for h in range(n_heads):
        ot = acc_ref[2 * h] / l_ref[2 * h] - lam * (acc_ref[2 * h + 1] / l_ref[2 * h + 1])
        ms = jnp.mean(ot * ot, axis=0, keepdims=True)
        normed.append((ot * lax.rsqrt(ms + SUBLN_EPS)).T * g_ref[...] * (1.0 - lam_init))
    y_ref[...] = _finish_rows(normed, sz_ref[...], gp_ref[...], x_ref[...], wout_ref, lng_ref, lnb_ref, alpha)


def _attn_prompt(q, kb, vt, bias_tab, lams, gp, sz, x, w_out, subln_g, ln_g, ln_b, lam_init, alpha, n_heads):
    B, S, D = x.shape
    T = ATTN_TILE
    assert S % T == 0 and T >= MAX_DISTANCE
    d_attn = q.shape[-1]
    row_blk = lambda width: pl.BlockSpec((None, T, width), lambda b, i: (b, i, 0))
    seq_blk = pl.BlockSpec((None, S, d_attn), lambda b, i: (b, 0, 0))
    vt_blk = pl.BlockSpec((None, S // T, d_attn, T), lambda b, i: (b, 0, 0, 0))
    const = lambda a: pl.BlockSpec(a.shape, lambda b, i: (0,) * a.ndim)
    return pl.pallas_call(
        functools.partial(_attn_prompt_kernel, T=T, n_heads=n_heads, lam_init=lam_init, alpha=alpha),
        grid=(B, S // T),
        in_specs=[row_blk(d_attn), seq_blk, vt_blk, const(bias_tab)] + [const(l) for l in lams]
                 + [row_blk(gp.shape[-1]), row_blk(d_attn), row_blk(D), const(w_out), const(subln_g),
                    const(ln_g), const(ln_b)],
        out_specs=row_blk(D),
        out_shape=jax.ShapeDtypeStruct((B, S, D), F32),
        scratch_shapes=[pltpu.VMEM((2 * n_heads, 1, T), F32), pltpu.VMEM((2 * n_heads, 1, T), F32),
                        pltpu.VMEM((2 * n_heads, V_DIM, T), F32), pltpu.VMEM((T, 2 * T), F32)],
        compiler_params=_compiler_params(2),
        name="attn_prompt",
    )(q, kb, vt, bias_tab, *lams, gp, sz, x, w_out, subln_g, ln_g, ln_b)


def _deinterleave_heads(page_ref, rows, n_heads):
    return jnp.concatenate(
        [page_ref[pl.ds(h, rows, stride=n_heads), :].astype(BF16) for h in range(n_heads)], axis=-1)


def _attn_decode_kernel(pt_ref, q_ref, sk_ref, sv_ref, bias_ref, biasn_ref,
                        lq1_ref, lk1_ref, lq2_ref, lk2_ref, ck_hbm, cv_hbm,
                        o_ref, kbuf, vbuf, sem, qbd_ref, m_ref, l_ref, acc_ref,
                        *, P, page, L, n_heads, n_chunks, n_batch, lam_init):
    d_attn = n_heads * V_DIM
    n_steps = n_batch * n_chunks
    t = pl.program_id(0)
    slot = lax.rem(t, 2)
    c = lax.rem(t, n_chunks)

    def page_copies(t, slot):
        copies = []
        for i in range(P):
            pg = pt_ref[t * P + i]
            copies.append(pltpu.make_async_copy(ck_hbm.at[pg], kbuf.at[slot, i], sem.at[0, slot]))
            copies.append(pltpu.make_async_copy(cv_hbm.at[pg], vbuf.at[slot, i], sem.at[1, slot]))
        return copies

    def update(k_all, v_all, bias):
        s = lax.dot_general(qbd_ref[...], k_all, NT_DIMS, preferred_element_type=F32) + bias
        m_old = m_ref[...]
        m_new = jnp.maximum(m_old, jnp.max(s, axis=-1, keepdims=True))
        a = jnp.exp(m_old - m_new)
        p = jnp.exp(s - m_new)
        l_ref[...] = a * l_ref[...] + jnp.sum(p, axis=-1, keepdims=True)
        acc_ref[...] = a * acc_ref[...] + jnp.dot(p.astype(BF16), v_all, preferred_element_type=F32)
        m_ref[...] = m_new

    @pl.when(t == 0)
    def _():
        for cp in page_copies(0, 0):
            cp.start()

    @pl.when(t + 1 < n_steps)
    def _():
        for cp in page_copies(t + 1, 1 - slot):
            cp.start()

    for cp in page_copies(t, slot):
        cp.wait()

    @pl.when(c == 0)
    def _():
        q = q_ref[...]
        lane = lax.broadcasted_iota(jnp.int32, (L, d_attn), 1)
        rows = []
        for h in range(n_heads):
            for cc in range(2):
                lo = h * V_DIM + cc * HEAD_DIM
                rows.append(jnp.where((lane >= lo) & (lane < lo + HEAD_DIM), q, 0.0))
        qbd_ref[...] = jnp.concatenate(rows, axis=0).astype(BF16)
        m_ref[...] = jnp.full(m_ref.shape, -jnp.inf, F32)
        l_ref[...] = jnp.zeros(l_ref.shape, F32)
        acc_ref[...] = jnp.zeros(acc_ref.shape, F32)

    k_all = jnp.concatenate([_deinterleave_heads(kbuf.at[slot, i], page, n_heads) for i in range(P)], axis=0)
    v_all = jnp.concatenate([_deinterleave_heads(vbuf.at[slot, i], page, n_heads) for i in range(P)], axis=0)
    update(k_all, v_all, bias_ref[c])

    @pl.when(c == n_chunks - 1)
    def _():
        pad = jnp.zeros((LANES - L, d_attn), BF16)
        update(jnp.concatenate([sk_ref[...].astype(BF16), pad], axis=0),
               jnp.concatenate([sv_ref[...].astype(BF16), pad], axis=0), biasn_ref[...])
        lam = _lam_value(lq1_ref[...], lk1_ref[...], lq2_ref[...], lk2_ref[...], lam_init)
        out = acc_ref[...] / l_ref[...]
        heads = []
        for h in range(n_heads):
            r0 = 2 * h * L
            cols = slice(h * V_DIM, (h + 1) * V_DIM)
            heads.append(out[r0:r0 + L, cols] - lam * out[r0 + L:r0 + 2 * L, cols])
        o_ref[...] = jnp.concatenate(heads, axis=-1)


def _attn_decode(page_table, q, sk, sv, cache_k, cache_v, bias_past, bias_new, lams, lam_init, n_heads):
    DB, L, d_attn = q.shape
    n_pages = page_table.shape[1]
    rows_per_page = cache_k.shape[1]
    page = rows_per_page // n_heads
    P = DECODE_PAGES
    assert n_pages % P == 0 and L == SUBLANES
    n_rows = 2 * n_heads * L
    n_chunks = n_pages // P
    whole = lambda a: pl.BlockSpec(a.shape, lambda t: (0,) * a.ndim)
    tok_blk = pl.BlockSpec((None, L, d_attn), lambda t: (t // n_chunks, 0, 0))
    in_specs = ([pl.BlockSpec(memory_space=pltpu.SMEM), tok_blk, tok_blk, tok_blk,
                 whole(bias_past), whole(bias_new)] + [whole(l) for l in lams]
                + [pl.BlockSpec(memory_space=pl.ANY)] * 2)
    return pl.pallas_call(
        functools.partial(_attn_decode_kernel, P=P, page=page, L=L, n_heads=n_heads,
                          n_chunks=n_chunks, n_batch=DB, lam_init=lam_init),
        grid=(DB * n_chunks,),
        in_specs=in_specs,
        out_specs=tok_blk,
        out_shape=jax.ShapeDtypeStruct((DB, L, d_attn), F32),
        scratch_shapes=[pltpu.VMEM((2, P, rows_per_page, V_DIM), F32),
                        pltpu.VMEM((2, P, rows_per_page, V_DIM), F32),
                        pltpu.SemaphoreType.DMA((2, 2)),
                        pltpu.VMEM((n_rows, d_attn), BF16), pltpu.VMEM((n_rows, 1), F32),
                        pltpu.VMEM((n_rows, 1), F32), pltpu.VMEM((n_rows, d_attn), F32)],
        compiler_params=_compiler_params(1),
        name="attn_decode",
    )(page_table.reshape(-1), q, sk, sv, bias_past, bias_new, *lams, cache_k, cache_v)


def _finish_sample_kernel(o_ref, sz_ref, gp_ref, x_ref, wout_ref, g_ref, lng_ref, lnb_ref, y_ref,
                          *, n_heads, lam_init, alpha):
    o = o_ref[...]
    normed = []
    for h in range(n_heads):
        oh = o[:, h * V_DIM:(h + 1) * V_DIM]
        ms = jnp.mean(oh * oh, axis=-1, keepdims=True)
        normed.append(oh * lax.rsqrt(ms + SUBLN_EPS) * g_ref[...] * (1.0 - lam_init))
    y_ref[...] = _finish_rows(normed, sz_ref[...], gp_ref[...], x_ref[...], wout_ref, lng_ref, lnb_ref, alpha)


def _finish_sample(o, sz, gp, x2d, w_out, subln_g, ln_g, ln_b, lam_init, alpha, n_heads):
    rows_total, D = x2d.shape
    rows = min(PROJ_ROWS, rows_total)
    assert rows_total % rows == 0
    row_blk = lambda a: pl.BlockSpec((rows, a.shape[-1]), lambda i: (i, 0))
    const = lambda a: pl.BlockSpec(a.shape, lambda i: (0,) * a.ndim)
    return pl.pallas_call(
        functools.partial(_finish_sample_kernel, n_heads=n_heads, lam_init=lam_init, alpha=alpha),
        grid=(rows_total // rows,),
        in_specs=[row_blk(o), row_blk(sz), row_blk(gp), row_blk(x2d), const(w_out), const(subln_g),
                  const(ln_g), const(ln_b)],
        out_specs=row_blk(x2d),
        out_shape=jax.ShapeDtypeStruct((rows_total, D), F32),
        compiler_params=_compiler_params(1),
        name="finish_sample",
    )(o, sz, gp, x2d, w_out, subln_g, ln_g, ln_b)


_N_LOG_BUCKETS = NUM_BUCKETS - MAX_EXACT
_LOG_BUCKET_STARTS = tuple(math.ceil(MAX_EXACT * (MAX_DISTANCE / MAX_EXACT) ** (k / _N_LOG_BUCKETS))
                           for k in range(1, _N_LOG_BUCKETS))


def _bucket(dist):
    large = MAX_EXACT + sum((dist >= st).astype(jnp.int32) for st in _LOG_BUCKET_STARTS)
    return jnp.where(dist < MAX_EXACT, dist, large)


def _bias_of_dist(dist, table):
    bucket = _bucket(jnp.maximum(dist, 0))[None]
    t = table.astype(F32)
    b = jnp.zeros((t.shape[1],) + dist.shape, F32)
    for i in range(NUM_BUCKETS):
        b = jnp.where(bucket == i, t[i][:, None, None], b)
    return jnp.where((dist >= 0)[None], b, NEG_INF)


def _prompt_bias_tiles(table, T):
    i = jnp.arange(T, dtype=jnp.int32)
    base = i[None, :] - i[:, None]
    far = _bias_of_dist(jnp.full((1, 1), MAX_DISTANCE, jnp.int32), table)
    return jnp.stack([(_bias_of_dist(base + off * T, table) - far) * LOG2E for off in range(2)])


def kernel(x_prompt, x_sample, cache_k, cache_v, state_pool, page_table, w_in, pool_w, pool_scale,
           lambda_q1, lambda_k1, lambda_q2, lambda_k2, subln_g, rel_bias, w_out, ln_g, ln_b):
    B, S, D = x_prompt.shape
    DB, L, _ = x_sample.shape
    depth = w_in.shape[0]
    n_pool_pages, page, n_heads, hd2 = cache_k.shape[1:]
    assert hd2 == V_DIM
    n_pages = page_table.shape[1]
    past_len = n_pages * page
    d_attn = n_heads * V_DIM
    d_pool = pool_scale.shape[-1]
    assert d_pool == len(POOL_WINDOWS) * LANES and w_in.shape[-1] == 2 * d_pool + 4 * d_attn
    alpha = (2 * depth) ** 0.25

    bias_tab = _prompt_bias_tiles(rel_bias, ATTN_TILE)
    qpos_s = past_len + jnp.arange(L, dtype=jnp.int32)
    kpos_s = jnp.arange(past_len + L, dtype=jnp.int32)
    bias_s = _bias_of_dist(qpos_s[:, None] - kpos_s[None, :], rel_bias)
    bias_s = jnp.broadcast_to(bias_s[:, None], (n_heads, 2, L, past_len + L)).reshape(2 * n_heads * L, -1)
    bias_past = bias_s[:, :past_len].reshape(bias_s.shape[0], -1, DECODE_PAGES * page).transpose(1, 0, 2)
    bias_new = jnp.pad(bias_s[:, past_len:], ((0, 0), (0, LANES - L)), constant_values=NEG_INF)

    xp = x_prompt
    xs = x_sample.reshape(DB * L, D)
    outs = [[] for _ in range(6)]
    for layer in range(depth):
        lam_init = _lambda_init(layer)
        w_in_b = w_in[layer].astype(BF16)
        w_out_b = w_out[layer].astype(BF16)
        pool_w_b = pool_w[layer].astype(BF16)
        ps = pool_scale[layer][None]
        lams = [v[layer][None] for v in (lambda_q1, lambda_k1, lambda_q2, lambda_k2)]
        g, lg, lb = subln_g[layer][None], ln_g[layer][None], ln_b[layer][None]
        ck = cache_k[layer].reshape(n_pool_pages, page * n_heads, V_DIM)
        cv = cache_v[layer].reshape(n_pool_pages, page * n_heads, V_DIM)

        k_p, v_p, q_p, kb_p, vt_p, gp_p, sz_p, pp = _proj_prompt(xp, w_in_b, pool_w_b, ps, d_pool, d_attn, n_heads)
        yp = _attn_prompt(q_p, kb_p, vt_p, bias_tab, lams, gp_p, sz_p, xp, w_out_b, g, lg, lb,
                          lam_init, alpha, n_heads)

        k_s, v_s, q_s, gp_s, sz_s, ps_new = _proj_sample(xs, state_pool[layer], w_in_b, pool_w_b, ps, L,
                                                        past_len, d_pool, d_attn, n_heads)
        o_s = _attn_decode(page_table, q_s.reshape(DB, L, d_attn),
                           k_s.reshape(DB, L, d_attn), v_s.reshape(DB, L, d_attn),
                           ck, cv, bias_past, bias_new, lams, lam_init, n_heads)
        ys = _finish_sample(o_s.reshape(DB * L, d_attn), sz_s, gp_s, xs, w_out_b, g, lg, lb,
                            lam_init, alpha, n_heads)

        outs[0].append(k_p.reshape(B, S, n_heads, V_DIM))
        outs[1].append(v_p.reshape(B, S, n_heads, V_DIM))
        outs[2].append(pp[:, HALO - POOL_BUF:])
        outs[3].append(k_s.reshape(DB, L, n_heads, V_DIM))
        outs[4].append(v_s.reshape(DB, L, n_heads, V_DIM))
        outs[5].append(ps_new)
        xp, xs = yp, ys

    return (xp, xs.reshape(DB, L, D), *(jnp.stack(o) for o in outs))
```

```python
import functools
import math

import jax
import jax.numpy as jnp
from jax import lax
from jax.experimental import pallas as pl
from jax.experimental.pallas import tpu as pltpu

F32 = jnp.float32
BF16 = jnp.bfloat16

POOL_WINDOWS = (2, 4, 8, 16)
POOL_BUF = max(POOL_WINDOWS) - 1
HEAD_DIM = 64
V_DIM = 2 * HEAD_DIM
ATTN_SCALE = HEAD_DIM ** -0.5
LOG2E = math.log2(math.e)
NUM_BUCKETS = 32
MAX_DISTANCE = 128
MAX_EXACT = NUM_BUCKETS // 2
NEG_INF = -1e30
LN_EPS = 1e-5
SUBLN_EPS = 1e-5

LANES = 128
SUBLANES = 8
HALO = 2 * SUBLANES
VMEM_LIMIT_BYTES = 56 * 1024 * 1024

PROJ_ROWS = 512
ATTN_TILE = 256
DECODE_PAGES = 16
DECODE_EVERY = 2
DECODE_SPLIT = 1
NT_DIMS = (((1,), (1,)), ((), ()))


def _lambda_init(layer):
    return 0.8 - 0.6 * math.exp(-0.3 * layer)


def _silu(z):
    return z / (1.0 + jnp.exp(-z))


def _lam_value(lq1, lk1, lq2, lk2, lam_init):
    a = jnp.sum(lq1 * lk1, axis=-1, keepdims=True)
    b = jnp.sum(lq2 * lk2, axis=-1, keepdims=True)
    return jnp.exp(a) - jnp.exp(b) + lam_init


def _compiler_params(n_axes):
    return pltpu.CompilerParams(dimension_semantics=("arbitrary",) * n_axes,
                                vmem_limit_bytes=VMEM_LIMIT_BYTES)


def _pool_diffs(u, window_sum, cnt_of):
    diffs = []
    for g, w in enumerate(POOL_WINDOWS):
        sl = slice(g * LANES, (g + 1) * LANES)
        diffs.append((window_sum(g, w) / cnt_of(w) - u[:, sl]).astype(BF16))
    return diffs


def _pool_mix(diffs, pw_ref, ps_ref):
    outs = [jnp.dot(d, pw_ref[g], preferred_element_type=F32) for g, d in enumerate(diffs)]
    return jnp.concatenate(outs, axis=-1) * ps_ref[...]


def _store_heads_interleaved(dst_ref, val, rows, n_heads):
    for h in range(n_heads):
        dst_ref[pl.ds(h, rows, stride=n_heads), :] = val[:, h * V_DIM:(h + 1) * V_DIM]


def _proj_prompt_kernel(x_ref, w_ref, pw_ref, ps_ref,
                        k_ref, v_ref, q_ref, kb_ref, vt_ref, gp_ref, sz_ref, pp_ref,
                        ext_ref, *, bm, d_pool, d_attn, n_heads):
    s = pl.program_id(1)
    xb = x_ref[...].astype(BF16)
    offs = [0, d_pool, 2 * d_pool, 2 * d_pool + d_attn, 2 * d_pool + 2 * d_attn,
            2 * d_pool + 3 * d_attn, 2 * d_pool + 4 * d_attn]

    def proj(j):
        return jnp.dot(xb, w_ref[:, offs[j]:offs[j + 1]], preferred_element_type=F32)

    pu = proj(0)

    @pl.when(s == 0)
    def _():
        ext_ref[0:HALO, :] = jnp.zeros((HALO, d_pool), F32)

    ext_ref[HALO:HALO + bm, :] = pu
    pos = s * bm + lax.broadcasted_iota(jnp.int32, (bm, 1), 0)

    def window_sum(g, w):
        sl = slice(g * LANES, (g + 1) * LANES)
        acc = pu[:, sl]
        for sh in range(1, w):
            acc = acc + ext_ref[HALO - sh:HALO - sh + bm, sl]
        return acc

    def cnt_of(w):
        return jnp.minimum(pos + 1, w).astype(F32)

    diffs = _pool_diffs(pu, window_sum, cnt_of)
    ext_ref[0:HALO, :] = pu[bm - HALO:, :]
    gate = _silu(proj(1))
    q_ref[...] = (proj(2) * (ATTN_SCALE * LOG2E)).astype(BF16)
    k = proj(3)
    kb_ref[...] = k.astype(BF16)
    _store_heads_interleaved(k_ref, k, bm, n_heads)
    gp_ref[...] = (_pool_mix(diffs, pw_ref, ps_ref) * gate).astype(BF16)
    v = proj(4)
    for t in range(bm // ATTN_TILE):
        vt_ref[t] = v[t * ATTN_TILE:(t + 1) * ATTN_TILE, :].T.astype(BF16)
    _store_heads_interleaved(v_ref, v, bm, n_heads)
    sz_ref[...] = _silu(proj(5))

    @pl.when(s == pl.num_programs(1) - 1)
    def _():
        pp_ref[...] = pu[bm - HALO:, :]


def _proj_prompt(x, w_in, pool_w, pool_scale, d_pool, d_attn, n_heads):
    B, S, D = x.shape
    bm = PROJ_ROWS
    T = ATTN_TILE
    assert S % bm == 0 and bm >= HALO and bm % T == 0
    d_in = w_in.shape[1]
    row_blk = lambda width: pl.BlockSpec((None, bm, width), lambda b, s: (b, s, 0))
    const2 = lambda shape: pl.BlockSpec(shape, lambda b, s: (0,) * len(shape))
    out_shape = (
        jax.ShapeDtypeStruct((B, S * n_heads, V_DIM), F32),
        jax.ShapeDtypeStruct((B, S * n_heads, V_DIM), F32),
        jax.ShapeDtypeStruct((B, S, d_attn), BF16),
        jax.ShapeDtypeStruct((B, S, d_attn), BF16),
        jax.ShapeDtypeStruct((B, S // T, d_attn, T), BF16),
        jax.ShapeDtypeStruct((B, S, d_pool), BF16),
        jax.ShapeDtypeStruct((B, S, d_attn), F32),
        jax.ShapeDtypeStruct((B, HALO, d_pool), F32),
    )
    kv_blk = pl.BlockSpec((None, bm * n_heads, V_DIM), lambda b, s: (b, s, 0))
    vt_blk = pl.BlockSpec((None, bm // T, d_attn, T), lambda b, s: (b, s, 0, 0))
    out_specs = (kv_blk, kv_blk, row_blk(d_attn), row_blk(d_attn), vt_blk,
                 row_blk(d_pool), row_blk(d_attn),
                 pl.BlockSpec((None, HALO, d_pool), lambda b, s: (b, 0, 0)))
    return pl.pallas_call(
        functools.partial(_proj_prompt_kernel, bm=bm, d_pool=d_pool, d_attn=d_attn, n_heads=n_heads),
        grid=(B, S // bm),
        in_specs=[row_blk(D), const2((D, d_in)), const2(pool_w.shape), const2(pool_scale.shape)],
        out_specs=out_specs,
        out_shape=out_shape,
        scratch_shapes=[pltpu.VMEM((HALO + bm, d_pool), F32)],
        compiler_params=_compiler_params(2),
        name="proj_prompt",
    )(x, w_in, pool_w, pool_scale)


def _proj_sample_kernel(x_ref, st_ref, w_ref, pw_ref, ps_ref,
                        k_ref, v_ref, q_ref, gp_ref, sz_ref, ps_out_ref,
                        ext_ref, *, nb, L, pos0, d_pool, d_attn, n_heads):
    rows = nb * L
    xb = x_ref[...].astype(BF16)
    offs = [0, d_pool, 2 * d_pool, 2 * d_pool + d_attn, 2 * d_pool + 2 * d_attn,
            2 * d_pool + 3 * d_attn, 2 * d_pool + 4 * d_attn]

    def proj(j):
        return jnp.dot(xb, w_ref[:, offs[j]:offs[j + 1]], preferred_element_type=F32)

    su = proj(0)
    ext_ref[:, HALO - POOL_BUF:HALO, :] = st_ref[...]
    ext_ref[:, HALO:HALO + L, :] = su.reshape(nb, L, d_pool)
    pos = pos0 + lax.rem(lax.broadcasted_iota(jnp.int32, (rows, 1), 0), L)

    def window_sum(g, w):
        sl = slice(g * LANES, (g + 1) * LANES)
        acc = ext_ref[:, HALO:HALO + L, sl]
        for sh in range(1, w):
            acc = acc + ext_ref[:, HALO - sh:HALO - sh + L, sl]
        return acc.reshape(rows, LANES)

    def cnt_of(w):
        return jnp.minimum(pos + 1, w).astype(F32)

    pool_y = _pool_mix(_pool_diffs(su, window_sum, cnt_of), pw_ref, ps_ref)
    ps_out_ref[...] = ext_ref[:, HALO + L - POOL_BUF:HALO + L, :]

    gp_ref[...] = (pool_y * _silu(proj(1))).astype(BF16)
    q_ref[...] = proj(2) * ATTN_SCALE
    _store_heads_interleaved(k_ref, proj(3), rows, n_heads)
    _store_heads_interleaved(v_ref, proj(4), rows, n_heads)
    sz_ref[...] = _silu(proj(5))


def _proj_sample(x2d, state, w_in, pool_w, pool_scale, L, pos0, d_pool, d_attn, n_heads):
    rows_total, D = x2d.shape
    DB = state.shape[0]
    assert L == SUBLANES and rows_total == DB * L
    rows = min(PROJ_ROWS, rows_total)
    nb = rows // L
    assert rows_total % rows == 0
    d_in = w_in.shape[1]
    row_blk = lambda width: pl.BlockSpec((rows, width), lambda i: (i, 0))
    const = lambda shape: pl.BlockSpec(shape, lambda i: (0,) * len(shape))
    st_blk = pl.BlockSpec((nb, POOL_BUF, d_pool), lambda i: (i, 0, 0))
    kv_blk = pl.BlockSpec((rows * n_heads, V_DIM), lambda i: (i, 0))
    out_shape = (
        jax.ShapeDtypeStruct((rows_total * n_heads, V_DIM), F32),
        jax.ShapeDtypeStruct((rows_total * n_heads, V_DIM), F32),
        jax.ShapeDtypeStruct((rows_total, d_attn), F32),
        jax.ShapeDtypeStruct((rows_total, d_pool), BF16),
        jax.ShapeDtypeStruct((rows_total, d_attn), F32),
        jax.ShapeDtypeStruct((DB, POOL_BUF, d_pool), F32),
    )
    return pl.pallas_call(
        functools.partial(_proj_sample_kernel, nb=nb, L=L, pos0=pos0, d_pool=d_pool, d_attn=d_attn,
                          n_heads=n_heads),
        grid=(rows_total // rows,),
        in_specs=[row_blk(D), st_blk, const((D, d_in)), const(pool_w.shape), const(pool_scale.shape)],
        out_specs=(kv_blk, kv_blk, row_blk(d_attn), row_blk(d_pool), row_blk(d_attn), st_blk),
        out_shape=out_shape,
        scratch_shapes=[pltpu.VMEM((nb, HALO + L, d_pool), F32)],
        compiler_params=_compiler_params(1),
        name="proj_sample",
    )(x2d, state, w_in, pool_w, pool_scale)


def _finish_rows(normed_heads, sz, gp, x, wout_ref, lng_ref, lnb_ref, alpha):
    o = jnp.concatenate(normed_heads, axis=-1) * sz
    gated = jnp.concatenate([gp, o.astype(BF16)], axis=-1)
    h = jnp.dot(gated, wout_ref[...], preferred_element_type=F32)
    r = alpha * x + h
    mu = jnp.mean(r, axis=-1, keepdims=True)
    rc = r - mu
    var = jnp.mean(rc * rc, axis=-1, keepdims=True)
    return rc * lax.rsqrt(var + LN_EPS) * lng_ref[...] + lnb_ref[...]


def _deinterleave_heads(page_ref, rows, n_heads):
    return jnp.concatenate(
        [page_ref[pl.ds(h, rows, stride=n_heads), :].astype(BF16) for h in range(n_heads)], axis=-1)


def _attn_kernel(pt_ref, q_ref, kb_ref, vt_ref, bias_ref, lq1_ref, lk1_ref, lq2_ref, lk2_ref,
                 gp_ref, sz_ref, x_ref, wout_ref, g_ref, lng_ref, lnb_ref,
                 qs_ref, sk_ref, sv_ref, dbias_ref, dbiasn_ref, ck_hbm, cv_hbm,
                 y_ref, o_ref,
                 m_ref, l_ref, acc_ref, s0_ref, kbuf, vbuf, sem, qbd_ref, dm_ref, dl_ref, dacc_ref,
                 *, T, n_heads, lam_init, alpha, P, page, L, n_chunks, n_dec, nq, n_batch):
    b = pl.program_id(0)
    qi = pl.program_id(1)
    d_attn = n_heads * V_DIM
    lam = _lam_value(lq1_ref[...], lk1_ref[...], lq2_ref[...], lk2_ref[...], lam_init)

    def page_copies(t, slot):
        copies = []
        for i in range(P):
            pg = pt_ref[t * P + i]
            copies.append(pltpu.make_async_copy(ck_hbm.at[pg], kbuf.at[slot, i], sem.at[0, slot]))
            copies.append(pltpu.make_async_copy(cv_hbm.at[pg], vbuf.at[slot, i], sem.at[1, slot]))
        return copies

    def decode_scores(k_rows):
        return lax.dot_general(qbd_ref[...], k_rows, NT_DIMS, preferred_element_type=F32)

    def decode_update(i, s, v_rows):
        m_old = dm_ref[i]
        m_new = jnp.maximum(m_old, jnp.max(s, axis=-1, keepdims=True))
        a = jnp.exp(m_old - m_new)
        p = jnp.exp(s - m_new)
        dl_ref[i] = a * dl_ref[i] + jnp.sum(p, axis=-1, keepdims=True)
        dacc_ref[i] = a * dacc_ref[i] + jnp.dot(p.astype(BF16), v_rows, preferred_element_type=F32)
        dm_ref[i] = m_new

    def decode_step(t):
        slot = lax.rem(t, 2)

        @pl.when(t + 1 < n_dec)
        def _():
            for cp in page_copies(t + 1, 1 - slot):
                cp.start()

        for cp in page_copies(t, slot):
            cp.wait()

        bd = lax.div(t, n_chunks)
        c = lax.rem(t, n_chunks)

        @pl.when(c == 0)
        def _():
            q = qs_ref[bd]
            lane = lax.broadcasted_iota(jnp.int32, (L, d_attn), 1)
            rows = []
            for h in range(n_heads):
                for cc in range(2):
                    lo = h * V_DIM + cc * HEAD_DIM
                    rows.append(jnp.where((lane >= lo) & (lane < lo + HEAD_DIM), q, 0.0))
            qbd_ref[...] = jnp.concatenate(rows, axis=0).astype(BF16)
            dm_ref[...] = jnp.full(dm_ref.shape, -jnp.inf, F32)
            dl_ref[...] = jnp.zeros(dl_ref.shape, F32)
            dacc_ref[...] = jnp.zeros(dacc_ref.shape, F32)

        part = P // DECODE_SPLIT
        rows_of = lambda buf, i: jnp.concatenate(
            [_deinterleave_heads(buf.at[slot, i * part + r], page, n_heads) for r in range(part)], axis=0)
        s_parts = [decode_scores(rows_of(kbuf, i)) for i in range(DECODE_SPLIT)]
        for i in range(DECODE_SPLIT):
            bias = dbias_ref[c, :, i * part * page:(i + 1) * part * page]
            decode_update(i, s_parts[i] + bias, rows_of(vbuf, i))

        @pl.when(c == n_chunks - 1)
        def _():
            pad = jnp.zeros((LANES - L, d_attn), BF16)
            k_new = jnp.concatenate([sk_ref[bd].astype(BF16), pad], axis=0)
            v_new = jnp.concatenate([sv_ref[bd].astype(BF16), pad], axis=0)
            decode_update(0, decode_scores(k_new) + dbiasn_ref[...], v_new)
            m_all = dm_ref[0]
            for i in range(1, DECODE_SPLIT):
                m_all = jnp.maximum(m_all, dm_ref[i])
            l_all = jnp.zeros_like(m_all)
            acc_all = jnp.zeros(dacc_ref.shape[1:], F32)
            for i in range(DECODE_SPLIT):
                w = jnp.exp(dm_ref[i] - m_all)
                l_all = l_all + w * dl_ref[i]
                acc_all = acc_all + w * dacc_ref[i]
            out = acc_all / l_all
            heads = []
            for h in range(n_heads):
                r0 = 2 * h * L
                cols = slice(h * V_DIM, (h + 1) * V_DIM)
                heads.append(out[r0:r0 + L, cols] - lam * out[r0 + L:r0 + 2 * L, cols])
            o_ref[bd] = jnp.concatenate(heads, axis=-1)

    @pl.when((b == 0) & (qi == 0))
    def _():
        for cp in page_copies(0, 0):
            cp.start()

    q = q_ref[...].astype(F32)
    lane = lax.broadcasted_iota(jnp.int32, (T, V_DIM), 1)
    qq = []
    for h in range(n_heads):
        qh = q[:, h * V_DIM:(h + 1) * V_DIM]
        qq.append(jnp.concatenate([jnp.where(lane < HEAD_DIM, qh, 0.0),
                                   jnp.where(lane >= HEAD_DIM, qh, 0.0)], axis=0).astype(BF16))

    m_ref[...] = jnp.full(m_ref.shape, -jnp.inf, F32)
    l_ref[...] = jnp.zeros(l_ref.shape, F32)
    acc_ref[...] = jnp.zeros(acc_ref.shape, F32)

    def scores(j, h):
        kh = kb_ref[pl.ds(pl.multiple_of(j * T, T), T), :][:, h * V_DIM:(h + 1) * V_DIM]
        return lax.dot_general(kh, qq[h], NT_DIMS, preferred_element_type=F32)

    tiles_per_batch = nq * (nq + 1) // 2
    tile_base = b * tiles_per_batch + lax.div(qi * (qi + 1), 2)

    def kv_tile(j, near):
        u = tile_base + j
        t = lax.div(u, DECODE_EVERY)

        @pl.when((lax.rem(u, DECODE_EVERY) == 0) & (t < n_dec))
        def _():
            decode_step(t)

        vt = vt_ref[j]
        s_next = s0_ref[...]
        for h in range(n_heads):
            s12 = s_next
            s_next = scores(j, h + 1) if h + 1 < n_heads else scores(jnp.minimum(j + 1, qi), 0)
            vth = vt[h * V_DIM:(h + 1) * V_DIM, :]
            if near:
                bias = bias_ref[qi - j, h]
            for c in range(2):
                i = 2 * h + c
                s = s12[:, c * T:(c + 1) * T]
                if near:
                    s = s + bias
                m_old = m_ref[i]
                m_new = jnp.maximum(m_old, jnp.max(s, axis=0, keepdims=True))
                a = jnp.exp2(m_old - m_new)
                p = jnp.exp2(s - m_new)
                l_ref[i] = a * l_ref[i] + jnp.sum(p, axis=0, keepdims=True)
                acc_ref[i] = a * acc_ref[i] + jnp.dot(vth, p.astype(BF16), preferred_element_type=F32)
                m_ref[i] = m_new
        s0_ref[...] = s_next

    def far_body(j, carry):
        kv_tile(j, near=False)
        return carry

    def near_body(j, carry):
        kv_tile(j, near=True)
        return carry

    @pl.when(qi >= 0)
    def _():
        s0_ref[...] = scores(0, 0)

    n_far = jnp.maximum(qi - 1, 0)
    lax.fori_loop(0, n_far, far_body, 0)
    lax.fori_loop(n_far, qi + 1, near_body, 0)

    normed = []
    for h in range(n_heads):
        ot = acc_ref[2 * h] / l_ref[2 * h] - lam * (acc_ref[2 * h + 1] / l_ref[2 * h + 1])
        ms = jnp.mean(ot * ot, axis=0, keepdims=True)
        normed.append((ot * lax.rsqrt(ms + SUBLN_EPS)).T * g_ref[...] * (1.0 - lam_init))
    y_ref[...] = _finish_rows(normed, sz_ref[...], gp_ref[...], x_ref[...], wout_ref, lng_ref, lnb_ref, alpha)

    first_left = -(-(n_batch * tiles_per_batch) // DECODE_EVERY)
    if first_left < n_dec:
        @pl.when((b == n_batch - 1) & (qi == nq - 1))
        def _():
            def tail_body(t, carry):
                decode_step(t)
                return carry
            lax.fori_loop(first_left, n_dec, tail_body, 0)


def _attn(page_table, q, kb, vt, bias_tab, lams, gp, sz, x, w_out, subln_g, ln_g, ln_b,
          qs, sk, sv, cache_k, cache_v, bias_past, bias_new, lam_init, alpha, n_heads):
    B, S, D = x.shape
    T = ATTN_TILE
    assert S % T == 0 and T >= MAX_DISTANCE
    nq = S // T
    d_attn = q.shape[-1]
    DB, L, _ = qs.shape
    n_pages = page_table.shape[1]
    rows_per_page = cache_k.shape[1]
    page = rows_per_page // n_heads
    P = DECODE_PAGES
    assert n_pages % P == 0 and P % DECODE_SPLIT == 0 and L == SUBLANES
    n_chunks = n_pages // P
    n_rows = 2 * n_heads * L
    row_blk = lambda width: pl.BlockSpec((None, T, width), lambda b, i: (b, i, 0))
    seq_blk = pl.BlockSpec((None, S, d_attn), lambda b, i: (b, 0, 0), pipeline_mode=pl.Buffered(1))
    vt_blk = pl.BlockSpec((None, nq, d_attn, T), lambda b, i: (b, 0, 0, 0), pipeline_mode=pl.Buffered(1))
    const = lambda a: pl.BlockSpec(a.shape, lambda b, i: (0,) * a.ndim)
    any_spec = pl.BlockSpec(memory_space=pl.ANY)
    in_specs = ([pl.BlockSpec(memory_space=pltpu.SMEM), row_blk(d_attn), seq_blk, vt_blk, const(bias_tab)]
                + [const(l) for l in lams]
                + [row_blk(gp.shape[-1]), row_blk(d_attn), row_blk(D), const(w_out), const(subln_g),
                   const(ln_g), const(ln_b),
                   const(qs), const(sk), const(sv), const(bias_past), const(bias_new), any_spec, any_spec])
    return pl.pallas_call(
        functools.partial(_attn_kernel, T=T, n_heads=n_heads, lam_init=lam_init, alpha=alpha, P=P, page=page,
                          L=L, n_chunks=n_chunks, n_dec=DB * n_chunks, nq=nq, n_batch=B),
        grid=(B, nq),
        in_specs=in_specs,
        out_specs=(row_blk(D), const(qs)),
        out_shape=(jax.ShapeDtypeStruct((B, S, D), F32), jax.ShapeDtypeStruct((DB, L, d_attn), F32)),
        scratch_shapes=[pltpu.VMEM((2 * n_heads, 1, T), F32), pltpu.VMEM((2 * n_heads, 1, T), F32),
                        pltpu.VMEM((2 * n_heads, V_DIM, T), F32), pltpu.VMEM((T, 2 * T), F32),
                        pltpu.VMEM((2, P, rows_per_page, V_DIM), F32),
                        pltpu.VMEM((2, P, rows_per_page, V_DIM), F32),
                        pltpu.SemaphoreType.DMA((2, 2)),
                        pltpu.VMEM((n_rows, d_attn), BF16), pltpu.VMEM((DECODE_SPLIT, n_rows, 1), F32),
                        pltpu.VMEM((DECODE_SPLIT, n_rows, 1), F32),
                        pltpu.VMEM((DECODE_SPLIT, n_rows, d_attn), F32)],
        compiler_params=_compiler_params(2),
        name="attn",
    )(page_table.reshape(-1), q, kb, vt, bias_tab, *lams, gp, sz, x, w_out, subln_g, ln_g, ln_b,
      qs, sk, sv, bias_past, bias_new, cache_k, cache_v)


def _finish_sample_kernel(o_ref, sz_ref, gp_ref, x_ref, wout_ref, g_ref, lng_ref, lnb_ref, y_ref,
                          *, n_heads, lam_init, alpha):
    o = o_ref[...]
    normed = []
    for h in range(n_heads):
        oh = o[:, h * V_DIM:(h + 1) * V_DIM]
        ms = jnp.mean(oh * oh, axis=-1, keepdims=True)
        normed.append(oh * lax.rsqrt(ms + SUBLN_EPS) * g_ref[...] * (1.0 - lam_init))
    y_ref[...] = _finish_rows(normed, sz_ref[...], gp_ref[...], x_ref[...], wout_ref, lng_ref, lnb_ref, alpha)


def _finish_sample(o, sz, gp, x2d, w_out, subln_g, ln_g, ln_b, lam_init, alpha, n_heads):
    rows_total, D = x2d.shape
    rows = min(PROJ_ROWS, rows_total)
    assert rows_total % rows == 0
    row_blk = lambda a: pl.BlockSpec((rows, a.shape[-1]), lambda i: (i, 0))
    const = lambda a: pl.BlockSpec(a.shape, lambda i: (0,) * a.ndim)
    return pl.pallas_call(
        functools.partial(_finish_sample_kernel, n_heads=n_heads, lam_init=lam_init, alpha=alpha),
        grid=(rows_total // rows,),
        in_specs=[row_blk(o), row_blk(sz), row_blk(gp), row_blk(x2d), const(w_out), const(subln_g),
                  const(ln_g), const(ln_b)],
        out_specs=row_blk(x2d),
        out_shape=jax.ShapeDtypeStruct((rows_total, D), F32),
        compiler_params=_compiler_params(1),
        name="finish_sample",
    )(o, sz, gp, x2d, w_out, subln_g, ln_g, ln_b)


_N_LOG_BUCKETS = NUM_BUCKETS - MAX_EXACT
_LOG_BUCKET_STARTS = tuple(math.ceil(MAX_EXACT * (MAX_DISTANCE / MAX_EXACT) ** (k / _N_LOG_BUCKETS))
                           for k in range(1, _N_LOG_BUCKETS))


def _bucket(dist):
    large = MAX_EXACT + sum((dist >= st).astype(jnp.int32) for st in _LOG_BUCKET_STARTS)
    return jnp.where(dist < MAX_EXACT, dist, large)


def _bias_of_dist(dist, table):
    bucket = _bucket(jnp.maximum(dist, 0))[None]
    t = table.astype(F32)
    b = jnp.zeros((t.shape[1],) + dist.shape, F32)
    for i in range(NUM_BUCKETS):
        b = jnp.where(bucket == i, t[i][:, None, None], b)
    return jnp.where((dist >= 0)[None], b, NEG_INF)


def _prompt_bias_tiles(table, T):
    i = jnp.arange(T, dtype=jnp.int32)
    base = i[None, :] - i[:, None]
    far = _bias_of_dist(jnp.full((1, 1), MAX_DISTANCE, jnp.int32), table)
    return jnp.stack([(_bias_of_dist(base + off * T, table) - far) * LOG2E for off in range(2)])


def kernel(x_prompt, x_sample, cache_k, cache_v, state_pool, page_table, w_in, pool_w, pool_scale,
           lambda_q1, lambda_k1, lambda_q2, lambda_k2, subln_g, rel_bias, w_out, ln_g, ln_b):
    B, S, D = x_prompt.shape
    DB, L, _ = x_sample.shape
    depth = w_in.shape[0]
    n_pool_pages, page, n_heads, hd2 = cache_k.shape[1:]
    assert hd2 == V_DIM
    n_pages = page_table.shape[1]
    past_len = n_pages * page
    d_attn = n_heads * V_DIM
    d_pool = pool_scale.shape[-1]
    assert d_pool == len(POOL_WINDOWS) * LANES and w_in.shape[-1] == 2 * d_pool + 4 * d_attn
    alpha = (2 * depth) ** 0.25

    bias_tab = _prompt_bias_tiles(rel_bias, ATTN_TILE)
    qpos_s = past_len + jnp.arange(L, dtype=jnp.int32)
    kpos_s = jnp.arange(past_len + L, dtype=jnp.int32)
    bias_s = _bias_of_dist(qpos_s[:, None] - kpos_s[None, :], rel_bias)
    bias_s = jnp.broadcast_to(bias_s[:, None], (n_heads, 2, L, past_len + L)).reshape(2 * n_heads * L, -1)
    bias_past = bias_s[:, :past_len].reshape(bias_s.shape[0], -1, DECODE_PAGES * page).transpose(1, 0, 2)
    bias_new = jnp.pad(bias_s[:, past_len:], ((0, 0), (0, LANES - L)), constant_values=NEG_INF)

    xp = x_prompt
    xs = x_sample.reshape(DB * L, D)
    outs = [[] for _ in range(6)]
    for layer in range(depth):
        lam_init = _lambda_init(layer)
        w_in_b = w_in[layer].astype(BF16)
        w_out_b = w_out[layer].astype(BF16)
        pool_w_b = pool_w[layer].astype(BF16)
        ps = pool_scale[layer][None]
        lams = [v[layer][None] for v in (lambda_q1, lambda_k1, lambda_q2, lambda_k2)]
        g, lg, lb = subln_g[layer][None], ln_g[layer][None], ln_b[layer][None]
        ck = cache_k[layer].reshape(n_pool_pages, page * n_heads, V_DIM)
        cv = cache_v[layer].reshape(n_pool_pages, page * n_heads, V_DIM)

        k_p, v_p, q_p, kb_p, vt_p, gp_p, sz_p, pp = _proj_prompt(xp, w_in_b, pool_w_b, ps, d_pool, d_attn, n_heads)
        k_s, v_s, q_s, gp_s, sz_s, ps_new = _proj_sample(xs, state_pool[layer], w_in_b, pool_w_b, ps, L,
                                                        past_len, d_pool, d_attn, n_heads)
        yp, o_s = _attn(page_table, q_p, kb_p, vt_p, bias_tab, lams, gp_p, sz_p, xp, w_out_b, g, lg, lb,
                        q_s.reshape(DB, L, d_attn), k_s.reshape(DB, L, d_attn), v_s.reshape(DB, L, d_attn),
                        ck, cv, bias_past, bias_new, lam_init, alpha, n_heads)
        ys = _finish_sample(o_s.reshape(DB * L, d_attn), sz_s, gp_s, xs, w_out_b, g, lg, lb,
                            lam_init, alpha, n_heads)

        outs[0].append(k_p.reshape(B, S, n_heads, V_DIM))
        outs[1].append(v_p.reshape(B, S, n_heads, V_DIM))
        outs[2].append(pp[:, HALO - POOL_BUF:])
        outs[3].append(k_s.reshape(DB, L, n_heads, V_DIM))
        outs[4].append(v_s.reshape(DB, L, n_heads, V_DIM))
        outs[5].append(ps_new)
        xp, xs = yp, ys

    return (xp, xs.reshape(DB, L, D), *(jnp.stack(o) for o in outs))
```

```python
import functools
import math

import jax
import jax.numpy as jnp
from jax import lax
from jax.experimental import pallas as pl
from jax.experimental.pallas import tpu as pltpu

F32 = jnp.float32
BF16 = jnp.bfloat16

POOL_WINDOWS = (2, 4, 8, 16)
POOL_BUF = max(POOL_WINDOWS) - 1
HEAD_DIM = 64
V_DIM = 2 * HEAD_DIM
ATTN_SCALE = HEAD_DIM ** -0.5
LOG2E = math.log2(math.e)
NUM_BUCKETS = 32
MAX_DISTANCE = 128
MAX_EXACT = NUM_BUCKETS // 2
NEG_INF = -1e30
LN_EPS = 1e-5
SUBLN_EPS = 1e-5

LANES = 128
SUBLANES = 8
HALO = 2 * SUBLANES
VMEM_LIMIT_BYTES = 56 * 1024 * 1024

PROJ_ROWS = 512
ATTN_TILE = 256
DECODE_PAGES = 8
DECODE_SLOTS = 4
NT_DIMS = (((1,), (1,)), ((), ()))


def _lambda_init(layer):
    return 0.8 - 0.6 * math.exp(-0.3 * layer)


def _silu(z):
    return z / (1.0 + jnp.exp(-z))


def _lam_value(lq1, lk1, lq2, lk2, lam_init):
    a = jnp.sum(lq1 * lk1, axis=-1, keepdims=True)
    b = jnp.sum(lq2 * lk2, axis=-1, keepdims=True)
    return jnp.exp(a) - jnp.exp(b) + lam_init


def _compiler_params(n_axes):
    return pltpu.CompilerParams(dimension_semantics=("arbitrary",) * n_axes,
                                vmem_limit_bytes=VMEM_LIMIT_BYTES)


def _pool_diffs(u, window_sum, cnt_of):
    diffs = []
    for g, w in enumerate(POOL_WINDOWS):
        sl = slice(g * LANES, (g + 1) * LANES)
        diffs.append((window_sum(g, w) / cnt_of(w) - u[:, sl]).astype(BF16))
    return diffs


def _pool_mix(diffs, pw_ref, ps_ref):
    outs = [jnp.dot(d, pw_ref[g], preferred_element_type=F32) for g, d in enumerate(diffs)]
    return jnp.concatenate(outs, axis=-1) * ps_ref[...]


def _store_heads_interleaved(dst_ref, val, rows, n_heads):
    for h in range(n_heads):
        dst_ref[pl.ds(h, rows, stride=n_heads), :] = val[:, h * V_DIM:(h + 1) * V_DIM]


def _proj_prompt_kernel(x_ref, w_ref, pw_ref, ps_ref,
                        k_ref, v_ref, q_ref, kb_ref, vt_ref, gp_ref, sz_ref, pp_ref,
                        ext_ref, *, bm, d_pool, d_attn, n_heads):
    s = pl.program_id(1)
    xb = x_ref[...].astype(BF16)
    offs = [0, d_pool, 2 * d_pool, 2 * d_pool + d_attn, 2 * d_pool + 2 * d_attn,
            2 * d_pool + 3 * d_attn, 2 * d_pool + 4 * d_attn]

    def proj(j):
        return jnp.dot(xb, w_ref[:, offs[j]:offs[j + 1]], preferred_element_type=F32)

    pu = proj(0)

    @pl.when(s == 0)
    def _():
        ext_ref[0:HALO, :] = jnp.zeros((HALO, d_pool), F32)

    ext_ref[HALO:HALO + bm, :] = pu
    pos = s * bm + lax.broadcasted_iota(jnp.int32, (bm, 1), 0)

    def window_sum(g, w):
        sl = slice(g * LANES, (g + 1) * LANES)
        acc = pu[:, sl]
        for sh in range(1, w):
            acc = acc + ext_ref[HALO - sh:HALO - sh + bm, sl]
        return acc

    def cnt_of(w):
        return jnp.minimum(pos + 1, w).astype(F32)

    diffs = _pool_diffs(pu, window_sum, cnt_of)
    ext_ref[0:HALO, :] = pu[bm - HALO:, :]
    gate = _silu(proj(1))
    q_ref[...] = (proj(2) * (ATTN_SCALE * LOG2E)).astype(BF16)
    k = proj(3)
    kb_ref[...] = k.astype(BF16)
    _store_heads_interleaved(k_ref, k, bm, n_heads)
    gp_ref[...] = (_pool_mix(diffs, pw_ref, ps_ref) * gate).astype(BF16)
    v = proj(4)
    for t in range(bm // ATTN_TILE):
        vt_ref[t] = v[t * ATTN_TILE:(t + 1) * ATTN_TILE, :].T.astype(BF16)
    _store_heads_interleaved(v_ref, v, bm, n_heads)
    sz_ref[...] = _silu(proj(5))

    @pl.when(s == pl.num_programs(1) - 1)
    def _():
        pp_ref[...] = pu[bm - HALO:, :]


def _proj_prompt(x, w_in, pool_w, pool_scale, d_pool, d_attn, n_heads):
    B, S, D = x.shape
    bm = PROJ_ROWS
    T = ATTN_TILE
    assert S % bm == 0 and bm >= HALO and bm % T == 0
    d_in = w_in.shape[1]
    row_blk = lambda width: pl.BlockSpec((None, bm, width), lambda b, s: (b, s, 0))
    const2 = lambda shape: pl.BlockSpec(shape, lambda b, s: (0,) * len(shape))
    out_shape = (
        jax.ShapeDtypeStruct((B, S * n_heads, V_DIM), F32),
        jax.ShapeDtypeStruct((B, S * n_heads, V_DIM), F32),
        jax.ShapeDtypeStruct((B, S, d_attn), BF16),
        jax.ShapeDtypeStruct((B, S, d_attn), BF16),
        jax.ShapeDtypeStruct((B, S // T, d_attn, T), BF16),
        jax.ShapeDtypeStruct((B, S, d_pool), BF16),
        jax.ShapeDtypeStruct((B, S, d_attn), F32),
        jax.ShapeDtypeStruct((B, HALO, d_pool), F32),
    )
    kv_blk = pl.BlockSpec((None, bm * n_heads, V_DIM), lambda b, s: (b, s, 0))
    vt_blk = pl.BlockSpec((None, bm // T, d_attn, T), lambda b, s: (b, s, 0, 0))
    out_specs = (kv_blk, kv_blk, row_blk(d_attn), row_blk(d_attn), vt_blk,
                 row_blk(d_pool), row_blk(d_attn),
                 pl.BlockSpec((None, HALO, d_pool), lambda b, s: (b, 0, 0)))
    return pl.pallas_call(
        functools.partial(_proj_prompt_kernel, bm=bm, d_pool=d_pool, d_attn=d_attn, n_heads=n_heads),
        grid=(B, S // bm),
        in_specs=[row_blk(D), const2((D, d_in)), const2(pool_w.shape), const2(pool_scale.shape)],
        out_specs=out_specs,
        out_shape=out_shape,
        scratch_shapes=[pltpu.VMEM((HALO + bm, d_pool), F32)],
        compiler_params=_compiler_params(2),
        name="proj_prompt",
    )(x, w_in, pool_w, pool_scale)


def _proj_sample_kernel(x_ref, st_ref, w_ref, pw_ref, ps_ref,
                        k_ref, v_ref, q_ref, gp_ref, sz_ref, ps_out_ref,
                        ext_ref, *, nb, L, pos0, d_pool, d_attn, n_heads):
    rows = nb * L
    xb = x_ref[...].astype(BF16)
    offs = [0, d_pool, 2 * d_pool, 2 * d_pool + d_attn, 2 * d_pool + 2 * d_attn,
            2 * d_pool + 3 * d_attn, 2 * d_pool + 4 * d_attn]

    def proj(j):
        return jnp.dot(xb, w_ref[:, offs[j]:offs[j + 1]], preferred_element_type=F32)

    su = proj(0)
    ext_ref[:, HALO - POOL_BUF:HALO, :] = st_ref[...]
    ext_ref[:, HALO:HALO + L, :] = su.reshape(nb, L, d_pool)
    pos = pos0 + lax.rem(lax.broadcasted_iota(jnp.int32, (rows, 1), 0), L)

    def window_sum(g, w):
        sl = slice(g * LANES, (g + 1) * LANES)
        acc = ext_ref[:, HALO:HALO + L, sl]
        for sh in range(1, w):
            acc = acc + ext_ref[:, HALO - sh:HALO - sh + L, sl]
        return acc.reshape(rows, LANES)

    def cnt_of(w):
        return jnp.minimum(pos + 1, w).astype(F32)

    pool_y = _pool_mix(_pool_diffs(su, window_sum, cnt_of), pw_ref, ps_ref)
    ps_out_ref[...] = ext_ref[:, HALO + L - POOL_BUF:HALO + L, :]

    gp_ref[...] = (pool_y * _silu(proj(1))).astype(BF16)
    q_ref[...] = proj(2) * ATTN_SCALE
    _store_heads_interleaved(k_ref, proj(3), rows, n_heads)
    _store_heads_interleaved(v_ref, proj(4), rows, n_heads)
    sz_ref[...] = _silu(proj(5))


def _proj_sample(x2d, state, w_in, pool_w, pool_scale, L, pos0, d_pool, d_attn, n_heads):
    rows_total, D = x2d.shape
    DB = state.shape[0]
    assert L == SUBLANES and rows_total == DB * L
    rows = min(PROJ_ROWS, rows_total)
    nb = rows // L
    assert rows_total % rows == 0
    d_in = w_in.shape[1]
    row_blk = lambda width: pl.BlockSpec((rows, width), lambda i: (i, 0))
    const = lambda shape: pl.BlockSpec(shape, lambda i: (0,) * len(shape))
    st_blk = pl.BlockSpec((nb, POOL_BUF, d_pool), lambda i: (i, 0, 0))
    kv_blk = pl.BlockSpec((rows * n_heads, V_DIM), lambda i: (i, 0))
    out_shape = (
        jax.ShapeDtypeStruct((rows_total * n_heads, V_DIM), F32),
        jax.ShapeDtypeStruct((rows_total * n_heads, V_DIM), F32),
        jax.ShapeDtypeStruct((rows_total, d_attn), F32),
        jax.ShapeDtypeStruct((rows_total, d_pool), BF16),
        jax.ShapeDtypeStruct((rows_total, d_attn), F32),
        jax.ShapeDtypeStruct((DB, POOL_BUF, d_pool), F32),
    )
    return pl.pallas_call(
        functools.partial(_proj_sample_kernel, nb=nb, L=L, pos0=pos0, d_pool=d_pool, d_attn=d_attn,
                          n_heads=n_heads),
        grid=(rows_total // rows,),
        in_specs=[row_blk(D), st_blk, const((D, d_in)), const(pool_w.shape), const(pool_scale.shape)],
        out_specs=(kv_blk, kv_blk, row_blk(d_attn), row_blk(d_pool), row_blk(d_attn), st_blk),
        out_shape=out_shape,
        scratch_shapes=[pltpu.VMEM((nb, HALO + L, d_pool), F32)],
        compiler_params=_compiler_params(1),
        name="proj_sample",
    )(x2d, state, w_in, pool_w, pool_scale)


def _finish_rows(normed_heads, sz, gp, x, wout_ref, lng_ref, lnb_ref, alpha):
    o = jnp.concatenate(normed_heads, axis=-1) * sz
    gated = jnp.concatenate([gp, o.astype(BF16)], axis=-1)
    h = jnp.dot(gated, wout_ref[...], preferred_element_type=F32)
    r = alpha * x + h
    mu = jnp.mean(r, axis=-1, keepdims=True)
    rc = r - mu
    var = jnp.mean(rc * rc, axis=-1, keepdims=True)
    return rc * lax.rsqrt(var + LN_EPS) * lng_ref[...] + lnb_ref[...]


def _deinterleave_heads(page_ref, rows, n_heads):
    return jnp.concatenate(
        [page_ref[pl.ds(h, rows, stride=n_heads), :].astype(BF16) for h in range(n_heads)], axis=-1)


def _attn_kernel(pt_ref, q_ref, kb_ref, vt_ref, bias_ref, lq1_ref, lk1_ref, lq2_ref, lk2_ref,
                 gp_ref, sz_ref, x_ref, wout_ref, g_ref, lng_ref, lnb_ref,
                 qs_ref, sk_ref, sv_ref, dbias_ref, dbiasn_ref, ck_hbm, cv_hbm,
                 y_ref, o_ref,
                 m_ref, l_ref, acc_ref, s0_ref, kbuf, vbuf, sem, qbd_ref, dm_ref, dl_ref, dacc_ref,
                 *, T, n_heads, lam_init, alpha, P, page, L, n_chunks, n_dec, nq, n_batch):
    b = pl.program_id(0)
    qi = pl.program_id(1)
    d_attn = n_heads * V_DIM
    lam = _lam_value(lq1_ref[...], lk1_ref[...], lq2_ref[...], lk2_ref[...], lam_init)

    def page_copies(t, slot):
        copies = []
        for i in range(P):
            pg = pt_ref[t * P + i]
            copies.append(pltpu.make_async_copy(ck_hbm.at[pg], kbuf.at[slot, i], sem.at[0, slot]))
            copies.append(pltpu.make_async_copy(cv_hbm.at[pg], vbuf.at[slot, i], sem.at[1, slot]))
        return copies

    def decode_rows(buf, slot):
        return jnp.concatenate([_deinterleave_heads(buf.at[slot, i], page, n_heads) for i in range(P)], axis=0)

    def decode_scores(k_rows):
        return lax.dot_general(qbd_ref[...], k_rows, NT_DIMS, preferred_element_type=F32)

    def decode_update(s, v_rows):
        m_old = dm_ref[...]
        m_new = jnp.maximum(m_old, jnp.max(s, axis=-1, keepdims=True))
        a = jnp.exp(m_old - m_new)
        p = jnp.exp(s - m_new)
        dl_ref[...] = a * dl_ref[...] + jnp.sum(p, axis=-1, keepdims=True)
        dacc_ref[...] = a * dacc_ref[...] + jnp.dot(p.astype(BF16), v_rows, preferred_element_type=F32)
        dm_ref[...] = m_new

    def decode_pre(t):
        slot = lax.rem(t, DECODE_SLOTS)
        ahead = t + (DECODE_SLOTS - 1)

        @pl.when(ahead < n_dec)
        def _():
            for cp in page_copies(ahead, lax.rem(ahead, DECODE_SLOTS)):
                cp.start()

        for cp in page_copies(t, slot):
            cp.wait()

        bd = lax.div(t, n_chunks)
        c = lax.rem(t, n_chunks)

        @pl.when(c == 0)
        def _():
            q = qs_ref[bd]
            lane = lax.broadcasted_iota(jnp.int32, (L, d_attn), 1)
            rows = []
            for h in range(n_heads):
                for cc in range(2):
                    lo = h * V_DIM + cc * HEAD_DIM
                    rows.append(jnp.where((lane >= lo) & (lane < lo + HEAD_DIM), q, 0.0))
            qbd_ref[...] = jnp.concatenate(rows, axis=0).astype(BF16)
            dm_ref[...] = jnp.full(dm_ref.shape, -jnp.inf, F32)
            dl_ref[...] = jnp.zeros(dl_ref.shape, F32)
            dacc_ref[...] = jnp.zeros(dacc_ref.shape, F32)

        return slot, bd, c

    def decode_post(bd, c):
        @pl.when(c == n_chunks - 1)
        def _():
            pad = jnp.zeros((LANES - L, d_attn), BF16)
            k_new = jnp.concatenate([sk_ref[bd].astype(BF16), pad], axis=0)
            v_new = jnp.concatenate([sv_ref[bd].astype(BF16), pad], axis=0)
            decode_update(decode_scores(k_new) + dbiasn_ref[...], v_new)
            out = dacc_ref[...] / dl_ref[...]
            heads = []
            for h in range(n_heads):
                r0 = 2 * h * L
                cols = slice(h * V_DIM, (h + 1) * V_DIM)
                heads.append(out[r0:r0 + L, cols] - lam * out[r0 + L:r0 + 2 * L, cols])
            o_ref[bd] = jnp.concatenate(heads, axis=-1)

    @pl.when((b == 0) & (qi == 0))
    def _():
        for t0 in range(min(DECODE_SLOTS - 1, n_dec)):
            for cp in page_copies(t0, t0):
                cp.start()

    q = q_ref[...].astype(F32)
    lane = lax.broadcasted_iota(jnp.int32, (T, V_DIM), 1)
    qq = []
    for h in range(n_heads):
        qh = q[:, h * V_DIM:(h + 1) * V_DIM]
        qq.append(jnp.concatenate([jnp.where(lane < HEAD_DIM, qh, 0.0),
                                   jnp.where(lane >= HEAD_DIM, qh, 0.0)], axis=0).astype(BF16))

    m_ref[...] = jnp.full(m_ref.shape, -jnp.inf, F32)
    l_ref[...] = jnp.zeros(l_ref.shape, F32)
    acc_ref[...] = jnp.zeros(acc_ref.shape, F32)

    def scores(j, h):
        kh = kb_ref[pl.ds(pl.multiple_of(j * T, T), T), :][:, h * V_DIM:(h + 1) * V_DIM]
        return lax.dot_general(kh, qq[h], NT_DIMS, preferred_element_type=F32)

    tiles_per_batch = nq * (nq + 1) // 2
    tile_base = b * tiles_per_batch + lax.div(qi * (qi + 1), 2)

    def kv_tile(j, near):
        slot, bd, dc = decode_pre(tile_base + j)
        s_dec = decode_scores(decode_rows(kbuf, slot)) + dbias_ref[dc]

        vt = vt_ref[j]
        s_next = s0_ref[...]
        for h in range(n_heads):
            if h == n_heads // 2:
                decode_update(s_dec, decode_rows(vbuf, slot))
            s12 = s_next
            s_next = scores(j, h + 1) if h + 1 < n_heads else scores(jnp.minimum(j + 1, qi), 0)
            vth = vt[h * V_DIM:(h + 1) * V_DIM, :]
            if near:
                bias = bias_ref[qi - j, h]
            for c in range(2):
                i = 2 * h + c
                s = s12[:, c * T:(c + 1) * T]
                if near:
                    s = s + bias
                m_old = m_ref[i]
                m_new = jnp.maximum(m_old, jnp.max(s, axis=0, keepdims=True))
                a = jnp.exp2(m_old - m_new)
                p = jnp.exp2(s - m_new)
                l_ref[i] = a * l_ref[i] + jnp.sum(p, axis=0, keepdims=True)
                acc_ref[i] = a * acc_ref[i] + jnp.dot(vth, p.astype(BF16), preferred_element_type=F32)
                m_ref[i] = m_new
        s0_ref[...] = s_next
        decode_post(bd, dc)

    def far_body(j, carry):
        kv_tile(j, near=False)
        return carry

    def near_body(j, carry):
        kv_tile(j, near=True)
        return carry

    @pl.when(qi >= 0)
    def _():
        s0_ref[...] = scores(0, 0)

    n_far = jnp.maximum(qi - 1, 0)
    lax.fori_loop(0, n_far, far_body, 0)
    lax.fori_loop(n_far, qi + 1, near_body, 0)

    normed = []
    for h in range(n_heads):
        ot = acc_ref[2 * h] / l_ref[2 * h] - lam * (acc_ref[2 * h + 1] / l_ref[2 * h + 1])
        ms = jnp.mean(ot * ot, axis=0, keepdims=True)
        normed.append((ot * lax.rsqrt(ms + SUBLN_EPS)).T * g_ref[...] * (1.0 - lam_init))
    y_ref[...] = _finish_rows(normed, sz_ref[...], gp_ref[...], x_ref[...], wout_ref, lng_ref, lnb_ref, alpha)

    first_left = n_batch * tiles_per_batch
    if first_left < n_dec:
        @pl.when((b == n_batch - 1) & (qi == nq - 1))
        def _():
            def tail_body(t, carry):
                slot, bd, dc = decode_pre(t)
                decode_update(decode_scores(decode_rows(kbuf, slot)) + dbias_ref[dc], decode_rows(vbuf, slot))
                decode_post(bd, dc)
                return carry
            lax.fori_loop(first_left, n_dec, tail_body, 0)


def _attn(page_table, q, kb, vt, bias_tab, lams, gp, sz, x, w_out, subln_g, ln_g, ln_b,
          qs, sk, sv, cache_k, cache_v, bias_past, bias_new, lam_init, alpha, n_heads):
    B, S, D = x.shape
    T = ATTN_TILE
    assert S % T == 0 and T >= MAX_DISTANCE
    nq = S // T
    d_attn = q.shape[-1]
    DB, L, _ = qs.shape
    n_pages = page_table.shape[1]
    rows_per_page = cache_k.shape[1]
    page = rows_per_page // n_heads
    P = DECODE_PAGES
    assert n_pages % P == 0 and L == SUBLANES
    n_chunks = n_pages // P
    n_rows = 2 * n_heads * L
    n_iters = B * (nq * (nq + 1) // 2)
    DBp = max(DB, -(-n_iters // n_chunks))
    pad_rows = lambda a: jnp.pad(a, ((0, DBp - DB),) + ((0, 0),) * (a.ndim - 1))
    page_table, qs, sk, sv = pad_rows(page_table), pad_rows(qs), pad_rows(sk), pad_rows(sv)
    row_blk = lambda width: pl.BlockSpec((None, T, width), lambda b, i: (b, i, 0))
    seq_blk = pl.BlockSpec((None, S, d_attn), lambda b, i: (b, 0, 0), pipeline_mode=pl.Buffered(1))
    vt_blk = pl.BlockSpec((None, nq, d_attn, T), lambda b, i: (b, 0, 0, 0), pipeline_mode=pl.Buffered(1))
    const = lambda a: pl.BlockSpec(a.shape, lambda b, i: (0,) * a.ndim)
    any_spec = pl.BlockSpec(memory_space=pl.ANY)
    in_specs = ([pl.BlockSpec(memory_space=pltpu.SMEM), row_blk(d_attn), seq_blk, vt_blk, const(bias_tab)]
                + [const(l) for l in lams]
                + [row_blk(gp.shape[-1]), row_blk(d_attn), row_blk(D), const(w_out), const(subln_g),
                   const(ln_g), const(ln_b),
                   const(qs), const(sk), const(sv), const(bias_past), const(bias_new), any_spec, any_spec])
    y, o = pl.pallas_call(
        functools.partial(_attn_kernel, T=T, n_heads=n_heads, lam_init=lam_init, alpha=alpha, P=P, page=page,
                          L=L, n_chunks=n_chunks, n_dec=DBp * n_chunks, nq=nq, n_batch=B),
        grid=(B, nq),
        in_specs=in_specs,
        out_specs=(row_blk(D), const(qs)),
        out_shape=(jax.ShapeDtypeStruct((B, S, D), F32), jax.ShapeDtypeStruct((DBp, L, d_attn), F32)),
        scratch_shapes=[pltpu.VMEM((2 * n_heads, 1, T), F32), pltpu.VMEM((2 * n_heads, 1, T), F32),
                        pltpu.VMEM((2 * n_heads, V_DIM, T), F32), pltpu.VMEM((T, 2 * T), F32),
                        pltpu.VMEM((DECODE_SLOTS, P, rows_per_page, V_DIM), F32),
                        pltpu.VMEM((DECODE_SLOTS, P, rows_per_page, V_DIM), F32),
                        pltpu.SemaphoreType.DMA((2, DECODE_SLOTS)),
                        pltpu.VMEM((n_rows, d_attn), BF16), pltpu.VMEM((n_rows, 1), F32),
                        pltpu.VMEM((n_rows, 1), F32), pltpu.VMEM((n_rows, d_attn), F32)],
        compiler_params=_compiler_params(2),
        name="attn",
    )(page_table.reshape(-1), q, kb, vt, bias_tab, *lams, gp, sz, x, w_out, subln_g, ln_g, ln_b,
      qs, sk, sv, bias_past, bias_new, cache_k, cache_v)
    return y, o[:DB]


def _finish_sample_kernel(o_ref, sz_ref, gp_ref, x_ref, wout_ref, g_ref, lng_ref, lnb_ref, y_ref,
                          *, n_heads, lam_init, alpha):
    o = o_ref[...]
    normed = []
    for h in range(n_heads):
        oh = o[:, h * V_DIM:(h + 1) * V_DIM]
        ms = jnp.mean(oh * oh, axis=-1, keepdims=True)
        normed.append(oh * lax.rsqrt(ms + SUBLN_EPS) * g_ref[...] * (1.0 - lam_init))
    y_ref[...] = _finish_rows(normed, sz_ref[...], gp_ref[...], x_ref[...], wout_ref, lng_ref, lnb_ref, alpha)


def _finish_sample(o, sz, gp, x2d, w_out, subln_g, ln_g, ln_b, lam_init, alpha, n_heads):
    rows_total, D = x2d.shape
    rows = min(PROJ_ROWS, rows_total)
    assert rows_total % rows == 0
    row_blk = lambda a: pl.BlockSpec((rows, a.shape[-1]), lambda i: (i, 0))
    const = lambda a: pl.BlockSpec(a.shape, lambda i: (0,) * a.ndim)
    return pl.pallas_call(
        functools.partial(_finish_sample_kernel, n_heads=n_heads, lam_init=lam_init, alpha=alpha),
        grid=(rows_total // rows,),
        in_specs=[row_blk(o), row_blk(sz), row_blk(gp), row_blk(x2d), const(w_out), const(subln_g),
                  const(ln_g), const(ln_b)],
        out_specs=row_blk(x2d),
        out_shape=jax.ShapeDtypeStruct((rows_total, D), F32),
        compiler_params=_compiler_params(1),
        name="finish_sample",
    )(o, sz, gp, x2d, w_out, subln_g, ln_g, ln_b)


_N_LOG_BUCKETS = NUM_BUCKETS - MAX_EXACT
_LOG_BUCKET_STARTS = tuple(math.ceil(MAX_EXACT * (MAX_DISTANCE / MAX_EXACT) ** (k / _N_LOG_BUCKETS))
                           for k in range(1, _N_LOG_BUCKETS))


def _bucket(dist):
    large = MAX_EXACT + sum((dist >= st).astype(jnp.int32) for st in _LOG_BUCKET_STARTS)
    return jnp.where(dist < MAX_EXACT, dist, large)


def _bias_of_dist(dist, table):
    bucket = _bucket(jnp.maximum(dist, 0))[None]
    t = table.astype(F32)
    b = jnp.zeros((t.shape[1],) + dist.shape, F32)
    for i in range(NUM_BUCKETS):
        b = jnp.where(bucket == i, t[i][:, None, None], b)
    return jnp.where((dist >= 0)[None], b, NEG_INF)


def _prompt_bias_tiles(table, T):
    i = jnp.arange(T, dtype=jnp.int32)
    base = i[None, :] - i[:, None]
    far = _bias_of_dist(jnp.full((1, 1), MAX_DISTANCE, jnp.int32), table)
    return jnp.stack([(_bias_of_dist(base + off * T, table) - far) * LOG2E for off in range(2)])


def kernel(x_prompt, x_sample, cache_k, cache_v, state_pool, page_table, w_in, pool_w, pool_scale,
           lambda_q1, lambda_k1, lambda_q2, lambda_k2, subln_g, rel_bias, w_out, ln_g, ln_b):
    B, S, D = x_prompt.shape
    DB, L, _ = x_sample.shape
    depth = w_in.shape[0]
    n_pool_pages, page, n_heads, hd2 = cache_k.shape[1:]
    assert hd2 == V_DIM
    n_pages = page_table.shape[1]
    past_len = n_pages * page
    d_attn = n_heads * V_DIM
    d_pool = pool_scale.shape[-1]
    assert d_pool == len(POOL_WINDOWS) * LANES and w_in.shape[-1] == 2 * d_pool + 4 * d_attn
    alpha = (2 * depth) ** 0.25

    bias_tab = _prompt_bias_tiles(rel_bias, ATTN_TILE)
    qpos_s = past_len + jnp.arange(L, dtype=jnp.int32)
    kpos_s = jnp.arange(past_len + L, dtype=jnp.int32)
    bias_s = _bias_of_dist(qpos_s[:, None] - kpos_s[None, :], rel_bias)
    bias_s = jnp.broadcast_to(bias_s[:, None], (n_heads, 2, L, past_len + L)).reshape(2 * n_heads * L, -1)
    bias_past = bias_s[:, :past_len].reshape(bias_s.shape[0], -1, DECODE_PAGES * page).transpose(1, 0, 2)
    bias_new = jnp.pad(bias_s[:, past_len:], ((0, 0), (0, LANES - L)), constant_values=NEG_INF)

    xp = x_prompt
    xs = x_sample.reshape(DB * L, D)
    outs = [[] for _ in range(6)]
    for layer in range(depth):
        lam_init = _lambda_init(layer)
        w_in_b = w_in[layer].astype(BF16)
        w_out_b = w_out[layer].astype(BF16)
        pool_w_b = pool_w[layer].astype(BF16)
        ps = pool_scale[layer][None]
        lams = [v[layer][None] for v in (lambda_q1, lambda_k1, lambda_q2, lambda_k2)]
        g, lg, lb = subln_g[layer][None], ln_g[layer][None], ln_b[layer][None]
        ck = cache_k[layer].reshape(n_pool_pages, page * n_heads, V_DIM)
        cv = cache_v[layer].reshape(n_pool_pages, page * n_heads, V_DIM)

        k_p, v_p, q_p, kb_p, vt_p, gp_p, sz_p, pp = _proj_prompt(xp, w_in_b, pool_w_b, ps, d_pool, d_attn, n_heads)
        k_s, v_s, q_s, gp_s, sz_s, ps_new = _proj_sample(xs, state_pool[layer], w_in_b, pool_w_b, ps, L,
                                                        past_len, d_pool, d_attn, n_heads)
        yp, o_s = _attn(page_table, q_p, kb_p, vt_p, bias_tab, lams, gp_p, sz_p, xp, w_out_b, g, lg, lb,
                        q_s.reshape(DB, L, d_attn), k_s.reshape(DB, L, d_attn), v_s.reshape(DB, L, d_attn),
                        ck, cv, bias_past, bias_new, lam_init, alpha, n_heads)
        ys = _finish_sample(o_s.reshape(DB * L, d_attn), sz_s, gp_s, xs, w_out_b, g, lg, lb,
                            lam_init, alpha, n_heads)

        outs[0].append(k_p.reshape(B, S, n_heads, V_DIM))
        outs[1].append(v_p.reshape(B, S, n_heads, V_DIM))
        outs[2].append(pp[:, HALO - POOL_BUF:])
        outs[3].append(k_s.reshape(DB, L, n_heads, V_DIM))
        outs[4].append(v_s.reshape(DB, L, n_heads, V_DIM))
        outs[5].append(ps_new)
        xp, xs = yp, ys

    return (xp, xs.reshape(DB, L, D), *(jnp.stack(o) for o in outs))
```

```python
import functools
import math

import jax
import jax.numpy as jnp
from jax import lax
from jax.experimental import pallas as pl
from jax.experimental.pallas import tpu as pltpu

F32 = jnp.float32
BF16 = jnp.bfloat16

POOL_WINDOWS = (2, 4, 8, 16)
POOL_BUF = max(POOL_WINDOWS) - 1
HEAD_DIM = 64
V_DIM = 2 * HEAD_DIM
ATTN_SCALE = HEAD_DIM ** -0.5
LOG2E = math.log2(math.e)
NUM_BUCKETS = 32
MAX_DISTANCE = 128
MAX_EXACT = NUM_BUCKETS // 2
NEG_INF = -1e30
LN_EPS = 1e-5
SUBLN_EPS = 1e-5

LANES = 128
SUBLANES = 8
HALO = 2 * SUBLANES
VT_PAD = 2 * SUBLANES
VT_ROWS = V_DIM + VT_PAD
VMEM_LIMIT_BYTES = 56 * 1024 * 1024

PROJ_ROWS = 512
ATTN_TILE = 256
DECODE_PAGES = 16
DECODE_EVERY = 2
DECODE_SLOTS = 2
NT_DIMS = (((1,), (1,)), ((), ()))


def _lambda_init(layer):
    return 0.8 - 0.6 * math.exp(-0.3 * layer)


def _silu(z):
    return z / (1.0 + jnp.exp(-z))


def _lam_value(lq1, lk1, lq2, lk2, lam_init):
    a = jnp.sum(lq1 * lk1, axis=-1, keepdims=True)
    b = jnp.sum(lq2 * lk2, axis=-1, keepdims=True)
    return jnp.exp(a) - jnp.exp(b) + lam_init


def _compiler_params(n_axes):
    return pltpu.CompilerParams(dimension_semantics=("arbitrary",) * n_axes,
                                vmem_limit_bytes=VMEM_LIMIT_BYTES)


def _pool_diffs(u, window_sum, cnt_of):
    diffs = []
    for g, w in enumerate(POOL_WINDOWS):
        sl = slice(g * LANES, (g + 1) * LANES)
        diffs.append((window_sum(g, w) / cnt_of(w) - u[:, sl]).astype(BF16))
    return diffs


def _pool_mix(diffs, pw_ref, ps_ref):
    outs = [jnp.dot(d, pw_ref[g], preferred_element_type=F32) for g, d in enumerate(diffs)]
    return jnp.concatenate(outs, axis=-1) * ps_ref[...]


def _store_heads_interleaved(dst_ref, val, rows, n_heads):
    for h in range(n_heads):
        dst_ref[pl.ds(h, rows, stride=n_heads), :] = val[:, h * V_DIM:(h + 1) * V_DIM]


def _proj_prompt_kernel(x_ref, w_ref, pw_ref, ps_ref,
                        k_ref, v_ref, q_ref, kb_ref, vt_ref, gp_ref, sz_ref, pp_ref,
                        ext_ref, *, bm, d_pool, d_attn, n_heads):
    s = pl.program_id(1)
    xb = x_ref[...].astype(BF16)
    offs = [0, d_pool, 2 * d_pool, 2 * d_pool + d_attn, 2 * d_pool + 2 * d_attn,
            2 * d_pool + 3 * d_attn, 2 * d_pool + 4 * d_attn]

    def proj(j):
        return jnp.dot(xb, w_ref[:, offs[j]:offs[j + 1]], preferred_element_type=F32)

    pu = proj(0)

    @pl.when(s == 0)
    def _():
        ext_ref[0:HALO, :] = jnp.zeros((HALO, d_pool), F32)

    ext_ref[HALO:HALO + bm, :] = pu
    pos = s * bm + lax.broadcasted_iota(jnp.int32, (bm, 1), 0)

    def window_sum(g, w):
        sl = slice(g * LANES, (g + 1) * LANES)
        acc = pu[:, sl]
        for sh in range(1, w):
            acc = acc + ext_ref[HALO - sh:HALO - sh + bm, sl]
        return acc

    def cnt_of(w):
        return jnp.minimum(pos + 1, w).astype(F32)

    diffs = _pool_diffs(pu, window_sum, cnt_of)
    ext_ref[0:HALO, :] = pu[bm - HALO:, :]
    gate = _silu(proj(1))
    q_ref[...] = (proj(2) * (ATTN_SCALE * LOG2E)).astype(BF16)
    k = proj(3)
    kb_ref[...] = k.astype(BF16)
    _store_heads_interleaved(k_ref, k, bm, n_heads)
    gp_ref[...] = (_pool_mix(diffs, pw_ref, ps_ref) * gate).astype(BF16)
    v = proj(4)
    ones_rows = (lax.broadcasted_iota(jnp.int32, (VT_PAD, ATTN_TILE), 0) == 0).astype(BF16)
    for t in range(bm // ATTN_TILE):
        v_tile = v[t * ATTN_TILE:(t + 1) * ATTN_TILE, :]
        for h in range(n_heads):
            r0 = h * VT_ROWS
            vt_ref[t, r0:r0 + V_DIM, :] = v_tile[:, h * V_DIM:(h + 1) * V_DIM].T.astype(BF16)
            vt_ref[t, r0 + V_DIM:r0 + VT_ROWS, :] = ones_rows
    _store_heads_interleaved(v_ref, v, bm, n_heads)
    sz_ref[...] = _silu(proj(5))

    @pl.when(s == pl.num_programs(1) - 1)
    def _():
        pp_ref[...] = pu[bm - HALO:, :]


def _proj_prompt(x, w_in, pool_w, pool_scale, d_pool, d_attn, n_heads):
    B, S, D = x.shape
    bm = PROJ_ROWS
    T = ATTN_TILE
    assert S % bm == 0 and bm >= HALO and bm % T == 0
    d_in = w_in.shape[1]
    row_blk = lambda width: pl.BlockSpec((None, bm, width), lambda b, s: (b, s, 0))
    const2 = lambda shape: pl.BlockSpec(shape, lambda b, s: (0,) * len(shape))
    out_shape = (
        jax.ShapeDtypeStruct((B, S * n_heads, V_DIM), F32),
        jax.ShapeDtypeStruct((B, S * n_heads, V_DIM), F32),
        jax.ShapeDtypeStruct((B, S, d_attn), BF16),
        jax.ShapeDtypeStruct((B, S, d_attn), BF16),
        jax.ShapeDtypeStruct((B, S // T, n_heads * VT_ROWS, T), BF16),
        jax.ShapeDtypeStruct((B, S, d_pool), BF16),
        jax.ShapeDtypeStruct((B, S, d_attn), F32),
        jax.ShapeDtypeStruct((B, HALO, d_pool), F32),
    )
    kv_blk = pl.BlockSpec((None, bm * n_heads, V_DIM), lambda b, s: (b, s, 0))
    vt_blk = pl.BlockSpec((None, bm // T, n_heads * VT_ROWS, T), lambda b, s: (b, s, 0, 0))
    out_specs = (kv_blk, kv_blk, row_blk(d_attn), row_blk(d_attn), vt_blk,
                 row_blk(d_pool), row_blk(d_attn),
                 pl.BlockSpec((None, HALO, d_pool), lambda b, s: (b, 0, 0)))
    return pl.pallas_call(
        functools.partial(_proj_prompt_kernel, bm=bm, d_pool=d_pool, d_attn=d_attn, n_heads=n_heads),
        grid=(B, S // bm),
        in_specs=[row_blk(D), const2((D, d_in)), const2(pool_w.shape), const2(pool_scale.shape)],
        out_specs=out_specs,
        out_shape=out_shape,
        scratch_shapes=[pltpu.VMEM((HALO + bm, d_pool), F32)],
        compiler_params=_compiler_params(2),
        name="proj_prompt",
    )(x, w_in, pool_w, pool_scale)


def _proj_sample_kernel(x_ref, st_ref, w_ref, pw_ref, ps_ref,
                        k_ref, v_ref, q_ref, gp_ref, sz_ref, ps_out_ref,
                        ext_ref, *, nb, L, pos0, d_pool, d_attn, n_heads):
    rows = nb * L
    xb = x_ref[...].astype(BF16)
    offs = [0, d_pool, 2 * d_pool, 2 * d_pool + d_attn, 2 * d_pool + 2 * d_attn,
            2 * d_pool + 3 * d_attn, 2 * d_pool + 4 * d_attn]

    def proj(j):
        return jnp.dot(xb, w_ref[:, offs[j]:offs[j + 1]], preferred_element_type=F32)

    su = proj(0)
    ext_ref[:, HALO - POOL_BUF:HALO, :] = st_ref[...]
    ext_ref[:, HALO:HALO + L, :] = su.reshape(nb, L, d_pool)
    pos = pos0 + lax.rem(lax.broadcasted_iota(jnp.int32, (rows, 1), 0), L)

    def window_sum(g, w):
        sl = slice(g * LANES, (g + 1) * LANES)
        acc = ext_ref[:, HALO:HALO + L, sl]
        for sh in range(1, w):
            acc = acc + ext_ref[:, HALO - sh:HALO - sh + L, sl]
        return acc.reshape(rows, LANES)

    def cnt_of(w):
        return jnp.minimum(pos + 1, w).astype(F32)

    pool_y = _pool_mix(_pool_diffs(su, window_sum, cnt_of), pw_ref, ps_ref)
    ps_out_ref[...] = ext_ref[:, HALO + L - POOL_BUF:HALO + L, :]

    gp_ref[...] = (pool_y * _silu(proj(1))).astype(BF16)
    q_ref[...] = proj(2) * ATTN_SCALE
    _store_heads_interleaved(k_ref, proj(3), rows, n_heads)
    _store_heads_interleaved(v_ref, proj(4), rows, n_heads)
    sz_ref[...] = _silu(proj(5))


def _proj_sample(x2d, state, w_in, pool_w, pool_scale, L, pos0, d_pool, d_attn, n_heads):
    rows_total, D = x2d.shape
    DB = state.shape[0]
    assert L == SUBLANES and rows_total == DB * L
    rows = min(PROJ_ROWS, rows_total)
    nb = rows // L
    assert rows_total % rows == 0
    d_in = w_in.shape[1]
    row_blk = lambda width: pl.BlockSpec((rows, width), lambda i: (i, 0))
    const = lambda shape: pl.BlockSpec(shape, lambda i: (0,) * len(shape))
    st_blk = pl.BlockSpec((nb, POOL_BUF, d_pool), lambda i: (i, 0, 0))
    kv_blk = pl.BlockSpec((rows * n_heads, V_DIM), lambda i: (i, 0))
    out_shape = (
        jax.ShapeDtypeStruct((rows_total * n_heads, V_DIM), F32),
        jax.ShapeDtypeStruct((rows_total * n_heads, V_DIM), F32),
        jax.ShapeDtypeStruct((rows_total, d_attn), F32),
        jax.ShapeDtypeStruct((rows_total, d_pool), BF16),
        jax.ShapeDtypeStruct((rows_total, d_attn), F32),
        jax.ShapeDtypeStruct((DB, POOL_BUF, d_pool), F32),
    )
    return pl.pallas_call(
        functools.partial(_proj_sample_kernel, nb=nb, L=L, pos0=pos0, d_pool=d_pool, d_attn=d_attn,
                          n_heads=n_heads),
        grid=(rows_total // rows,),
        in_specs=[row_blk(D), st_blk, const((D, d_in)), const(pool_w.shape), const(pool_scale.shape)],
        out_specs=(kv_blk, kv_blk, row_blk(d_attn), row_blk(d_pool), row_blk(d_attn), st_blk),
        out_shape=out_shape,
        scratch_shapes=[pltpu.VMEM((nb, HALO + L, d_pool), F32)],
        compiler_params=_compiler_params(1),
        name="proj_sample",
    )(x2d, state, w_in, pool_w, pool_scale)


def _finish_rows(normed_heads, sz, gp, x, wout_ref, lng_ref, lnb_ref, alpha):
    o = jnp.concatenate(normed_heads, axis=-1) * sz
    gated = jnp.concatenate([gp, o.astype(BF16)], axis=-1)
    h = jnp.dot(gated, wout_ref[...], preferred_element_type=F32)
    r = alpha * x + h
    mu = jnp.mean(r, axis=-1, keepdims=True)
    rc = r - mu
    var = jnp.mean(rc * rc, axis=-1, keepdims=True)
    return rc * lax.rsqrt(var + LN_EPS) * lng_ref[...] + lnb_ref[...]


def _deinterleave_heads(page_ref, rows, n_heads):
    return jnp.concatenate(
        [page_ref[pl.ds(h, rows, stride=n_heads), :].astype(BF16) for h in range(n_heads)], axis=-1)


def _attn_kernel(pt_ref, q_ref, kb_ref, vt_ref, bias_ref, lq1_ref, lk1_ref, lq2_ref, lk2_ref,
                 gp_ref, sz_ref, x_ref, wout_ref, g_ref, lng_ref, lnb_ref,
                 qs_ref, sk_ref, sv_ref, dbias_ref, dbiasn_ref, ck_hbm, cv_hbm,
                 y_ref, o_ref,
                 m_ref, acc_ref, s0_ref, kbuf, vbuf, sem, qbd_ref, dm_ref, dl_ref, dacc_ref,
                 *, T, n_heads, lam_init, alpha, P, page, L, n_chunks, n_dec, nq, n_batch):
    b = pl.program_id(0)
    qi = pl.program_id(1)
    d_attn = n_heads * V_DIM
    lam = _lam_value(lq1_ref[...], lk1_ref[...], lq2_ref[...], lk2_ref[...], lam_init)

    def page_copies(t, slot):
        copies = []
        for i in range(P):
            pg = pt_ref[t * P + i]
            copies.append(pltpu.make_async_copy(ck_hbm.at[pg], kbuf.at[slot, i], sem.at[0, slot]))
            copies.append(pltpu.make_async_copy(cv_hbm.at[pg], vbuf.at[slot, i], sem.at[1, slot]))
        return copies

    def decode_rows(buf, slot):
        return jnp.concatenate([_deinterleave_heads(buf.at[slot, i], page, n_heads) for i in range(P)], axis=0)

    def decode_scores(k_rows):
        return lax.dot_general(qbd_ref[...], k_rows, NT_DIMS, preferred_element_type=F32)

    def decode_update(s, v_rows):
        m_old = dm_ref[...]
        m_new = jnp.maximum(m_old, jnp.max(s, axis=-1, keepdims=True))
        a = jnp.exp(m_old - m_new)
        p = jnp.exp(s - m_new)
        dl_ref[...] = a * dl_ref[...] + jnp.sum(p, axis=-1, keepdims=True)
        dacc_ref[...] = a * dacc_ref[...] + jnp.dot(p.astype(BF16), v_rows, preferred_element_type=F32)
        dm_ref[...] = m_new

    def decode_pre(t):
        slot = lax.rem(t, DECODE_SLOTS)
        ahead = t + (DECODE_SLOTS - 1)

        @pl.when(ahead < n_dec)
        def _():
            for cp in page_copies(ahead, lax.rem(ahead, DECODE_SLOTS)):
                cp.start()

        for cp in page_copies(t, slot):
            cp.wait()

        bd = lax.div(t, n_chunks)
        c = lax.rem(t, n_chunks)

        @pl.when(c == 0)
        def _():
            q = qs_ref[bd]
            lane = lax.broadcasted_iota(jnp.int32, (L, d_attn), 1)
            rows = []
            for h in range(n_heads):
                for cc in range(2):
                    lo = h * V_DIM + cc * HEAD_DIM
                    rows.append(jnp.where((lane >= lo) & (lane < lo + HEAD_DIM), q, 0.0))
            qbd_ref[...] = jnp.concatenate(rows, axis=0).astype(BF16)
            dm_ref[...] = jnp.full(dm_ref.shape, -jnp.inf, F32)
            dl_ref[...] = jnp.zeros(dl_ref.shape, F32)
            dacc_ref[...] = jnp.zeros(dacc_ref.shape, F32)

        return slot, bd, c

    def decode_chunk(slot, bd, c):
        @pl.when(c < n_chunks - 1)
        def _():
            decode_update(decode_scores(decode_rows(kbuf, slot)) + dbias_ref[c], decode_rows(vbuf, slot))

        @pl.when(c == n_chunks - 1)
        def _():
            pad = jnp.zeros((LANES - L, d_attn), BF16)
            k_all = jnp.concatenate([decode_rows(kbuf, slot), sk_ref[bd].astype(BF16), pad], axis=0)
            v_all = jnp.concatenate([decode_rows(vbuf, slot), sv_ref[bd].astype(BF16), pad], axis=0)
            bias = jnp.concatenate([dbias_ref[c], dbiasn_ref[...]], axis=1)
            decode_update(decode_scores(k_all) + bias, v_all)
            out = dacc_ref[...] / dl_ref[...]
            heads = []
            for h in range(n_heads):
                r0 = 2 * h * L
                cols = slice(h * V_DIM, (h + 1) * V_DIM)
                heads.append(out[r0:r0 + L, cols] - lam * out[r0 + L:r0 + 2 * L, cols])
            o_ref[bd] = jnp.concatenate(heads, axis=-1)

    def decode_step(t):
        slot, bd, dc = decode_pre(t)
        decode_chunk(slot, bd, dc)

    @pl.when((b == 0) & (qi == 0))
    def _():
        for t0 in range(min(DECODE_SLOTS - 1, n_dec)):
            for cp in page_copies(t0, t0):
                cp.start()

    q = q_ref[...].astype(F32)
    lane = lax.broadcasted_iota(jnp.int32, (T, V_DIM), 1)
    qq = []
    for h in range(n_heads):
        qh = q[:, h * V_DIM:(h + 1) * V_DIM]
        qq.append(jnp.concatenate([jnp.where(lane < HEAD_DIM, qh, 0.0),
                                   jnp.where(lane >= HEAD_DIM, qh, 0.0)], axis=0).astype(BF16))

    m_ref[...] = jnp.full(m_ref.shape, -jnp.inf, F32)
    acc_ref[...] = jnp.zeros(acc_ref.shape, F32)

    def scores(j, h):
        kh = kb_ref[pl.ds(pl.multiple_of(j * T, T), T), :][:, h * V_DIM:(h + 1) * V_DIM]
        return lax.dot_general(kh, qq[h], NT_DIMS, preferred_element_type=F32)

    tiles_per_batch = nq * (nq + 1) // 2
    tile_base = b * tiles_per_batch + lax.div(qi * (qi + 1), 2)

    def kv_tile(j, near):
        u = tile_base + j
        t = lax.div(u, DECODE_EVERY)

        @pl.when((lax.rem(u, DECODE_EVERY) == 0) & (t < n_dec))
        def _():
            decode_step(t)

        vt = vt_ref[j]
        s_next = s0_ref[...]
        for h in range(n_heads):
            s12 = s_next
            s_next = scores(j, h + 1) if h + 1 < n_heads else scores(jnp.minimum(j + 1, qi), 0)
            vth = vt[h * VT_ROWS:(h + 1) * VT_ROWS, :]
            if near:
                bias = bias_ref[qi - j, h]
            for c in range(2):
                i = 2 * h + c
                s = s12[:, c * T:(c + 1) * T]
                if near:
                    s = s + bias
                m_old = m_ref[i]
                m_new = jnp.maximum(m_old, jnp.max(s, axis=0, keepdims=True))
                a = jnp.exp2(m_old - m_new)
                p = jnp.exp2(s - m_new)
                acc_ref[i] = a * acc_ref[i] + jnp.dot(vth, p.astype(BF16), preferred_element_type=F32)
                m_ref[i] = m_new
        s0_ref[...] = s_next

    def far_body(j, carry):
        kv_tile(j, near=False)
        return carry

    def near_body(j, carry):
        kv_tile(j, near=True)
        return carry

    @pl.when(qi >= 0)
    def _():
        s0_ref[...] = scores(0, 0)

    n_far = jnp.maximum(qi - 1, 0)
    lax.fori_loop(0, n_far, far_body, 0)
    lax.fori_loop(n_far, qi + 1, near_body, 0)

    def attended(i):
        acc = acc_ref[i]
        return acc[:V_DIM] / acc[V_DIM:V_DIM + 1]

    normed = []
    for h in range(n_heads):
        ot = attended(2 * h) - lam * attended(2 * h + 1)
        ms = jnp.mean(ot * ot, axis=0, keepdims=True)
        normed.append((ot * lax.rsqrt(ms + SUBLN_EPS)).T * g_ref[...] * (1.0 - lam_init))
    y_ref[...] = _finish_rows(normed, sz_ref[...], gp_ref[...], x_ref[...], wout_ref, lng_ref, lnb_ref, alpha)

    first_left = -(-(n_batch * tiles_per_batch) // DECODE_EVERY)
    if first_left < n_dec:
        @pl.when((b == n_batch - 1) & (qi == nq - 1))
        def _():
            def tail_body(t, carry):
                decode_step(t)
                return carry
            lax.fori_loop(first_left, n_dec, tail_body, 0)


def _attn(page_table, q, kb, vt, bias_tab, lams, gp, sz, x, w_out, subln_g, ln_g, ln_b,
          qs, sk, sv, cache_k, cache_v, bias_past, bias_new, lam_init, alpha, n_heads):
    B, S, D = x.shape
    T = ATTN_TILE
    assert S % T == 0 and T >= MAX_DISTANCE
    nq = S // T
    d_attn = q.shape[-1]
    DB, L, _ = qs.shape
    n_pages = page_table.shape[1]
    rows_per_page = cache_k.shape[1]
    page = rows_per_page // n_heads
    P = DECODE_PAGES
    assert n_pages % P == 0 and L == SUBLANES
    n_chunks = n_pages // P
    n_rows = 2 * n_heads * L
    row_blk = lambda width: pl.BlockSpec((None, T, width), lambda b, i: (b, i, 0))
    seq_blk = pl.BlockSpec((None, S, d_attn), lambda b, i: (b, 0, 0), pipeline_mode=pl.Buffered(1))
    vt_blk = pl.BlockSpec((None, nq, n_heads * VT_ROWS, T), lambda b, i: (b, 0, 0, 0),
                          pipeline_mode=pl.Buffered(1))
    const = lambda a: pl.BlockSpec(a.shape, lambda b, i: (0,) * a.ndim)
    any_spec = pl.BlockSpec(memory_space=pl.ANY)
    in_specs = ([pl.BlockSpec(memory_space=pltpu.SMEM), row_blk(d_attn), seq_blk, vt_blk, const(bias_tab)]
                + [const(l) for l in lams]
                + [row_blk(gp.shape[-1]), row_blk(d_attn), row_blk(D), const(w_out), const(subln_g),
                   const(ln_g), const(ln_b),
                   const(qs), const(sk), const(sv), const(bias_past), const(bias_new), any_spec, any_spec])
    y, o = pl.pallas_call(
        functools.partial(_attn_kernel, T=T, n_heads=n_heads, lam_init=lam_init, alpha=alpha, P=P, page=page,
                          L=L, n_chunks=n_chunks, n_dec=DB * n_chunks, nq=nq, n_batch=B),
        grid=(B, nq),
        in_specs=in_specs,
        out_specs=(row_blk(D), const(qs)),
        out_shape=(jax.ShapeDtypeStruct((B, S, D), F32), jax.ShapeDtypeStruct((DB, L, d_attn), F32)),
        scratch_shapes=[pltpu.VMEM((2 * n_heads, 1, T), F32),
                        pltpu.VMEM((2 * n_heads, VT_ROWS, T), F32), pltpu.VMEM((T, 2 * T), F32),
                        pltpu.VMEM((DECODE_SLOTS, P, rows_per_page, V_DIM), F32),
                        pltpu.VMEM((DECODE_SLOTS, P, rows_per_page, V_DIM), F32),
                        pltpu.SemaphoreType.DMA((2, DECODE_SLOTS)),
                        pltpu.VMEM((n_rows, d_attn), BF16), pltpu.VMEM((n_rows, 1), F32),
                        pltpu.VMEM((n_rows, 1), F32), pltpu.VMEM((n_rows, d_attn), F32)],
        compiler_params=_compiler_params(2),
        name="attn",
    )(page_table.reshape(-1), q, kb, vt, bias_tab, *lams, gp, sz, x, w_out, subln_g, ln_g, ln_b,
      qs, sk, sv, bias_past, bias_new, cache_k, cache_v)
    return y, o


def _finish_sample_kernel(o_ref, sz_ref, gp_ref, x_ref, wout_ref, g_ref, lng_ref, lnb_ref, y_ref,
                          *, n_heads, lam_init, alpha):
    o = o_ref[...]
    normed = []
    for h in range(n_heads):
        oh = o[:, h * V_DIM:(h + 1) * V_DIM]
        ms = jnp.mean(oh * oh, axis=-1, keepdims=True)
        normed.append(oh * lax.rsqrt(ms + SUBLN_EPS) * g_ref[...] * (1.0 - lam_init))
    y_ref[...] = _finish_rows(normed, sz_ref[...], gp_ref[...], x_ref[...], wout_ref, lng_ref, lnb_ref, alpha)


def _finish_sample(o, sz, gp, x2d, w_out, subln_g, ln_g, ln_b, lam_init, alpha, n_heads):
    rows_total, D = x2d.shape
    rows = min(PROJ_ROWS, rows_total)
    assert rows_total % rows == 0
    row_blk = lambda a: pl.BlockSpec((rows, a.shape[-1]), lambda i: (i, 0))
    const = lambda a: pl.BlockSpec(a.shape, lambda i: (0,) * a.ndim)
    return pl.pallas_call(
        functools.partial(_finish_sample_kernel, n_heads=n_heads, lam_init=lam_init, alpha=alpha),
        grid=(rows_total // rows,),
        in_specs=[row_blk(o), row_blk(sz), row_blk(gp), row_blk(x2d), const(w_out), const(subln_g),
                  const(ln_g), const(ln_b)],
        out_specs=row_blk(x2d),
        out_shape=jax.ShapeDtypeStruct((rows_total, D), F32),
        compiler_params=_compiler_params(1),
        name="finish_sample",
    )(o, sz, gp, x2d, w_out, subln_g, ln_g, ln_b)


_N_LOG_BUCKETS = NUM_BUCKETS - MAX_EXACT
_LOG_BUCKET_STARTS = tuple(math.ceil(MAX_EXACT * (MAX_DISTANCE / MAX_EXACT) ** (k / _N_LOG_BUCKETS))
                           for k in range(1, _N_LOG_BUCKETS))


def _bucket(dist):
    large = MAX_EXACT + sum((dist >= st).astype(jnp.int32) for st in _LOG_BUCKET_STARTS)
    return jnp.where(dist < MAX_EXACT, dist, large)


def _bias_of_dist(dist, table):
    bucket = _bucket(jnp.maximum(dist, 0))[None]
    t = table.astype(F32)
    b = jnp.zeros((t.shape[1],) + dist.shape, F32)
    for i in range(NUM_BUCKETS):
        b = jnp.where(bucket == i, t[i][:, None, None], b)
    return jnp.where((dist >= 0)[None], b, NEG_INF)


def _prompt_bias_tiles(table, T):
    i = jnp.arange(T, dtype=jnp.int32)
    base = i[None, :] - i[:, None]
    far = _bias_of_dist(jnp.full((1, 1), MAX_DISTANCE, jnp.int32), table)
    return jnp.stack([(_bias_of_dist(base + off * T, table) - far) * LOG2E for off in range(2)])


def kernel(x_prompt, x_sample, cache_k, cache_v, state_pool, page_table, w_in, pool_w, pool_scale,
           lambda_q1, lambda_k1, lambda_q2, lambda_k2, subln_g, rel_bias, w_out, ln_g, ln_b):
    B, S, D = x_prompt.shape
    DB, L, _ = x_sample.shape
    depth = w_in.shape[0]
    n_pool_pages, page, n_heads, hd2 = cache_k.shape[1:]
    assert hd2 == V_DIM
    n_pages = page_table.shape[1]
    past_len = n_pages * page
    d_attn = n_heads * V_DIM
    d_pool = pool_scale.shape[-1]
    assert d_pool == len(POOL_WINDOWS) * LANES and w_in.shape[-1] == 2 * d_pool + 4 * d_attn
    alpha = (2 * depth) ** 0.25

    bias_tab = _prompt_bias_tiles(rel_bias, ATTN_TILE)
    qpos_s = past_len + jnp.arange(L, dtype=jnp.int32)
    kpos_s = jnp.arange(past_len + L, dtype=jnp.int32)
    bias_s = _bias_of_dist(qpos_s[:, None] - kpos_s[None, :], rel_bias)
    bias_s = jnp.broadcast_to(bias_s[:, None], (n_heads, 2, L, past_len + L)).reshape(2 * n_heads * L, -1)
    bias_past = bias_s[:, :past_len].reshape(bias_s.shape[0], -1, DECODE_PAGES * page).transpose(1, 0, 2)
    bias_new = jnp.pad(bias_s[:, past_len:], ((0, 0), (0, LANES - L)), constant_values=NEG_INF)

    xp = x_prompt
    xs = x_sample.reshape(DB * L, D)
    outs = [[] for _ in range(6)]
    for layer in range(depth):
        lam_init = _lambda_init(layer)
        w_in_b = w_in[layer].astype(BF16)
        w_out_b = w_out[layer].astype(BF16)
        pool_w_b = pool_w[layer].astype(BF16)
        ps = pool_scale[layer][None]
        lams = [v[layer][None] for v in (lambda_q1, lambda_k1, lambda_q2, lambda_k2)]
        g, lg, lb = subln_g[layer][None], ln_g[layer][None], ln_b[layer][None]
        ck = cache_k[layer].reshape(n_pool_pages, page * n_heads, V_DIM)
        cv = cache_v[layer].reshape(n_pool_pages, page * n_heads, V_DIM)

        k_p, v_p, q_p, kb_p, vt_p, gp_p, sz_p, pp = _proj_prompt(xp, w_in_b, pool_w_b, ps, d_pool, d_attn, n_heads)
        k_s, v_s, q_s, gp_s, sz_s, ps_new = _proj_sample(xs, state_pool[layer], w_in_b, pool_w_b, ps, L,
                                                        past_len, d_pool, d_attn, n_heads)
        yp, o_s = _attn(page_table, q_p, kb_p, vt_p, bias_tab, lams, gp_p, sz_p, xp, w_out_b, g, lg, lb,
                        q_s.reshape(DB, L, d_attn), k_s.reshape(DB, L, d_attn), v_s.reshape(DB, L, d_attn),
                        ck, cv, bias_past, bias_new, lam_init, alpha, n_heads)
        ys = _finish_sample(o_s.reshape(DB * L, d_attn), sz_s, gp_s, xs, w_out_b, g, lg, lb,
                            lam_init, alpha, n_heads)

        outs[0].append(k_p.reshape(B, S, n_heads, V_DIM))
        outs[1].append(v_p.reshape(B, S, n_heads, V_DIM))
        outs[2].append(pp[:, HALO - POOL_BUF:])
        outs[3].append(k_s.reshape(DB, L, n_heads, V_DIM))
        outs[4].append(v_s.reshape(DB, L, n_heads, V_DIM))
        outs[5].append(ps_new)
        xp, xs = yp, ys

    return (xp, xs.reshape(DB, L, D), *(jnp.stack(o) for o in outs))
```

```python
import functools
import math

import jax
import jax.numpy as jnp
from jax import lax
from jax.experimental import pallas as pl
from jax.experimental.pallas import tpu as pltpu

F32 = jnp.float32
BF16 = jnp.bfloat16

POOL_WINDOWS = (2, 4, 8, 16)
POOL_BUF = max(POOL_WINDOWS) - 1
HEAD_DIM = 64
V_DIM = 2 * HEAD_DIM
ATTN_SCALE = HEAD_DIM ** -0.5
LOG2E = math.log2(math.e)
NUM_BUCKETS = 32
MAX_DISTANCE = 128
MAX_EXACT = NUM_BUCKETS // 2
NEG_INF = -1e30
LN_EPS = 1e-5
SUBLN_EPS = 1e-5

LANES = 128
SUBLANES = 8
HALO = 2 * SUBLANES
VT_PAD = 2 * SUBLANES
VT_ROWS = V_DIM + VT_PAD
VMEM_LIMIT_BYTES = 56 * 1024 * 1024

PROJ_ROWS = 512
ATTN_TILE = 256
DECODE_PAGES = 16
DECODE_EVERY = 2
DECODE_SLOTS = 2
FAR_UNROLL = 2
assert FAR_UNROLL <= DECODE_EVERY
NT_DIMS = (((1,), (1,)), ((), ()))


def _lambda_init(layer):
    return 0.8 - 0.6 * math.exp(-0.3 * layer)


def _silu(z):
    return z / (1.0 + jnp.exp(-z))


def _lam_value(lq1, lk1, lq2, lk2, lam_init):
    a = jnp.sum(lq1 * lk1, axis=-1, keepdims=True)
    b = jnp.sum(lq2 * lk2, axis=-1, keepdims=True)
    return jnp.exp(a) - jnp.exp(b) + lam_init


def _compiler_params(n_axes):
    return pltpu.CompilerParams(dimension_semantics=("arbitrary",) * n_axes,
                                vmem_limit_bytes=VMEM_LIMIT_BYTES)


def _pool_diffs(u, window_sum, cnt_of):
    diffs = []
    for g, w in enumerate(POOL_WINDOWS):
        sl = slice(g * LANES, (g + 1) * LANES)
        diffs.append((window_sum(g, w) / cnt_of(w) - u[:, sl]).astype(BF16))
    return diffs


def _pool_mix(diffs, pw_ref, ps_ref):
    outs = [jnp.dot(d, pw_ref[g], preferred_element_type=F32) for g, d in enumerate(diffs)]
    return jnp.concatenate(outs, axis=-1) * ps_ref[...]


def _store_heads_interleaved(dst_ref, val, rows, n_heads):
    for h in range(n_heads):
        dst_ref[pl.ds(h, rows, stride=n_heads), :] = val[:, h * V_DIM:(h + 1) * V_DIM]


def _proj_prompt_kernel(x_ref, w_ref, pw_ref, ps_ref,
                        k_ref, v_ref, q_ref, kb_ref, vt_ref, gp_ref, sz_ref, pp_ref,
                        ext_ref, *, bm, d_pool, d_attn, n_heads):
    s = pl.program_id(1)
    xb = x_ref[...].astype(BF16)
    offs = [0, d_pool, 2 * d_pool, 2 * d_pool + d_attn, 2 * d_pool + 2 * d_attn,
            2 * d_pool + 3 * d_attn, 2 * d_pool + 4 * d_attn]

    def proj(j):
        return jnp.dot(xb, w_ref[:, offs[j]:offs[j + 1]], preferred_element_type=F32)

    pu = proj(0)

    @pl.when(s == 0)
    def _():
        ext_ref[0:HALO, :] = jnp.zeros((HALO, d_pool), F32)

    ext_ref[HALO:HALO + bm, :] = pu
    pos = s * bm + lax.broadcasted_iota(jnp.int32, (bm, 1), 0)

    def window_sum(g, w):
        sl = slice(g * LANES, (g + 1) * LANES)
        acc = pu[:, sl]
        for sh in range(1, w):
            acc = acc + ext_ref[HALO - sh:HALO - sh + bm, sl]
        return acc

    def cnt_of(w):
        return jnp.minimum(pos + 1, w).astype(F32)

    diffs = _pool_diffs(pu, window_sum, cnt_of)
    ext_ref[0:HALO, :] = pu[bm - HALO:, :]
    gate = _silu(proj(1))
    q_ref[...] = (proj(2) * (ATTN_SCALE * LOG2E)).astype(BF16)
    k = proj(3)
    kb_ref[...] = k.astype(BF16)
    _store_heads_interleaved(k_ref, k, bm, n_heads)
    gp_ref[...] = (_pool_mix(diffs, pw_ref, ps_ref) * gate).astype(BF16)
    v = proj(4)
    ones_rows = (lax.broadcasted_iota(jnp.int32, (VT_PAD, ATTN_TILE), 0) == 0).astype(BF16)
    for t in range(bm // ATTN_TILE):
        v_tile = v[t * ATTN_TILE:(t + 1) * ATTN_TILE, :]
        for h in range(n_heads):
            r0 = h * VT_ROWS
            vt_ref[t, r0:r0 + V_DIM, :] = v_tile[:, h * V_DIM:(h + 1) * V_DIM].T.astype(BF16)
            vt_ref[t, r0 + V_DIM:r0 + VT_ROWS, :] = ones_rows
    _store_heads_interleaved(v_ref, v, bm, n_heads)
    sz_ref[...] = _silu(proj(5))

    @pl.when(s == pl.num_programs(1) - 1)
    def _():
        pp_ref[...] = pu[bm - HALO:, :]


def _proj_prompt(x, w_in, pool_w, pool_scale, d_pool, d_attn, n_heads):
    B, S, D = x.shape
    bm = PROJ_ROWS
    T = ATTN_TILE
    assert S % bm == 0 and bm >= HALO and bm % T == 0
    d_in = w_in.shape[1]
    row_blk = lambda width: pl.BlockSpec((None, bm, width), lambda b, s: (b, s, 0))
    const2 = lambda shape: pl.BlockSpec(shape, lambda b, s: (0,) * len(shape))
    out_shape = (
        jax.ShapeDtypeStruct((B, S * n_heads, V_DIM), F32),
        jax.ShapeDtypeStruct((B, S * n_heads, V_DIM), F32),
        jax.ShapeDtypeStruct((B, S, d_attn), BF16),
        jax.ShapeDtypeStruct((B, S, d_attn), BF16),
        jax.ShapeDtypeStruct((B, S // T, n_heads * VT_ROWS, T), BF16),
        jax.ShapeDtypeStruct((B, S, d_pool), BF16),
        jax.ShapeDtypeStruct((B, S, d_attn), F32),
        jax.ShapeDtypeStruct((B, HALO, d_pool), F32),
    )
    kv_blk = pl.BlockSpec((None, bm * n_heads, V_DIM), lambda b, s: (b, s, 0))
    vt_blk = pl.BlockSpec((None, bm // T, n_heads * VT_ROWS, T), lambda b, s: (b, s, 0, 0))
    out_specs = (kv_blk, kv_blk, row_blk(d_attn), row_blk(d_attn), vt_blk,
                 row_blk(d_pool), row_blk(d_attn),
                 pl.BlockSpec((None, HALO, d_pool), lambda b, s: (b, 0, 0)))
    return pl.pallas_call(
        functools.partial(_proj_prompt_kernel, bm=bm, d_pool=d_pool, d_attn=d_attn, n_heads=n_heads),
        grid=(B, S // bm),
        in_specs=[row_blk(D), const2((D, d_in)), const2(pool_w.shape), const2(pool_scale.shape)],
        out_specs=out_specs,
        out_shape=out_shape,
        scratch_shapes=[pltpu.VMEM((HALO + bm, d_pool), F32)],
        compiler_params=_compiler_params(2),
        name="proj_prompt",
    )(x, w_in, pool_w, pool_scale)


def _proj_sample_kernel(x_ref, st_ref, w_ref, pw_ref, ps_ref,
                        k_ref, v_ref, q_ref, gp_ref, sz_ref, ps_out_ref,
                        ext_ref, *, nb, L, pos0, d_pool, d_attn, n_heads):
    rows = nb * L
    xb = x_ref[...].astype(BF16)
    offs = [0, d_pool, 2 * d_pool, 2 * d_pool + d_attn, 2 * d_pool + 2 * d_attn,
            2 * d_pool + 3 * d_attn, 2 * d_pool + 4 * d_attn]

    def proj(j):
        return jnp.dot(xb, w_ref[:, offs[j]:offs[j + 1]], preferred_element_type=F32)

    su = proj(0)
    ext_ref[:, HALO - POOL_BUF:HALO, :] = st_ref[...]
    ext_ref[:, HALO:HALO + L, :] = su.reshape(nb, L, d_pool)
    pos = pos0 + lax.rem(lax.broadcasted_iota(jnp.int32, (rows, 1), 0), L)

    def window_sum(g, w):
        sl = slice(g * LANES, (g + 1) * LANES)
        acc = ext_ref[:, HALO:HALO + L, sl]
        for sh in range(1, w):
            acc = acc + ext_ref[:, HALO - sh:HALO - sh + L, sl]
        return acc.reshape(rows, LANES)

    def cnt_of(w):
        return jnp.minimum(pos + 1, w).astype(F32)

    pool_y = _pool_mix(_pool_diffs(su, window_sum, cnt_of), pw_ref, ps_ref)
    ps_out_ref[...] = ext_ref[:, HALO + L - POOL_BUF:HALO + L, :]

    gp_ref[...] = (pool_y * _silu(proj(1))).astype(BF16)
    q_ref[...] = proj(2) * ATTN_SCALE
    _store_heads_interleaved(k_ref, proj(3), rows, n_heads)
    _store_heads_interleaved(v_ref, proj(4), rows, n_heads)
    sz_ref[...] = _silu(proj(5))


def _proj_sample(x2d, state, w_in, pool_w, pool_scale, L, pos0, d_pool, d_attn, n_heads):
    rows_total, D = x2d.shape
    DB = state.shape[0]
    assert L == SUBLANES and rows_total == DB * L
    rows = min(PROJ_ROWS, rows_total)
    nb = rows // L
    assert rows_total % rows == 0
    d_in = w_in.shape[1]
    row_blk = lambda width: pl.BlockSpec((rows, width), lambda i: (i, 0))
    const = lambda shape: pl.BlockSpec(shape, lambda i: (0,) * len(shape))
    st_blk = pl.BlockSpec((nb, POOL_BUF, d_pool), lambda i: (i, 0, 0))
    kv_blk = pl.BlockSpec((rows * n_heads, V_DIM), lambda i: (i, 0))
    out_shape = (
        jax.ShapeDtypeStruct((rows_total * n_heads, V_DIM), F32),
        jax.ShapeDtypeStruct((rows_total * n_heads, V_DIM), F32),
        jax.ShapeDtypeStruct((rows_total, d_attn), F32),
        jax.ShapeDtypeStruct((rows_total, d_pool), BF16),
        jax.ShapeDtypeStruct((rows_total, d_attn), F32),
        jax.ShapeDtypeStruct((DB, POOL_BUF, d_pool), F32),
    )
    return pl.pallas_call(
        functools.partial(_proj_sample_kernel, nb=nb, L=L, pos0=pos0, d_pool=d_pool, d_attn=d_attn,
                          n_heads=n_heads),
        grid=(rows_total // rows,),
        in_specs=[row_blk(D), st_blk, const((D, d_in)), const(pool_w.shape), const(pool_scale.shape)],
        out_specs=(kv_blk, kv_blk, row_blk(d_attn), row_blk(d_pool), row_blk(d_attn), st_blk),
        out_shape=out_shape,
        scratch_shapes=[pltpu.VMEM((nb, HALO + L, d_pool), F32)],
        compiler_params=_compiler_params(1),
        name="proj_sample",
    )(x2d, state, w_in, pool_w, pool_scale)


def _finish_rows(normed_heads, sz, gp, x, wout_ref, lng_ref, lnb_ref, alpha):
    o = jnp.concatenate(normed_heads, axis=-1) * sz
    gated = jnp.concatenate([gp, o.astype(BF16)], axis=-1)
    h = jnp.dot(gated, wout_ref[...], preferred_element_type=F32)
    r = alpha * x + h
    mu = jnp.mean(r, axis=-1, keepdims=True)
    rc = r - mu
    var = jnp.mean(rc * rc, axis=-1, keepdims=True)
    return rc * lax.rsqrt(var + LN_EPS) * lng_ref[...] + lnb_ref[...]


def _deinterleave_heads(page_ref, rows, n_heads):
    return jnp.concatenate(
        [page_ref[pl.ds(h, rows, stride=n_heads), :].astype(BF16) for h in range(n_heads)], axis=-1)


def _attn_kernel(pt_ref, q_ref, kb_ref, vt_ref, bias_ref, lq1_ref, lk1_ref, lq2_ref, lk2_ref,
                 gp_ref, sz_ref, x_ref, wout_ref, g_ref, lng_ref, lnb_ref,
                 qs_ref, sk_ref, sv_ref, dbias_ref, dbiasn_ref, ck_hbm, cv_hbm,
                 y_ref, o_ref,
                 m_ref, acc_ref, s0_ref, kbuf, vbuf, sem, qbd_ref, dm_ref, dl_ref, dacc_ref,
                 *, T, n_heads, lam_init, alpha, P, page, L, n_chunks, n_dec, nq, n_batch):
    b = pl.program_id(0)
    qi = pl.program_id(1)
    d_attn = n_heads * V_DIM
    lam = _lam_value(lq1_ref[...], lk1_ref[...], lq2_ref[...], lk2_ref[...], lam_init)

    def page_copies(t, slot):
        copies = []
        for i in range(P):
            pg = pt_ref[t * P + i]
            copies.append(pltpu.make_async_copy(ck_hbm.at[pg], kbuf.at[slot, i], sem.at[0, slot]))
            copies.append(pltpu.make_async_copy(cv_hbm.at[pg], vbuf.at[slot, i], sem.at[1, slot]))
        return copies

    def decode_rows(buf, slot):
        return jnp.concatenate([_deinterleave_heads(buf.at[slot, i], page, n_heads) for i in range(P)], axis=0)

    def decode_scores(k_rows):
        return lax.dot_general(qbd_ref[...], k_rows, NT_DIMS, preferred_element_type=F32)

    def decode_update(s, v_rows):
        m_old = dm_ref[...]
        m_new = jnp.maximum(m_old, jnp.max(s, axis=-1, keepdims=True))
        a = jnp.exp(m_old - m_new)
        p = jnp.exp(s - m_new)
        dl_ref[...] = a * dl_ref[...] + jnp.sum(p, axis=-1, keepdims=True)
        dacc_ref[...] = a * dacc_ref[...] + jnp.dot(p.astype(BF16), v_rows, preferred_element_type=F32)
        dm_ref[...] = m_new

    def decode_pre(t):
        slot = lax.rem(t, DECODE_SLOTS)
        ahead = t + (DECODE_SLOTS - 1)

        @pl.when(ahead < n_dec)
        def _():
            for cp in page_copies(ahead, lax.rem(ahead, DECODE_SLOTS)):
                cp.start()

        for cp in page_copies(t, slot):
            cp.wait()

        bd = lax.div(t, n_chunks)
        c = lax.rem(t, n_chunks)

        @pl.when(c == 0)
        def _():
            q = qs_ref[bd]
            lane = lax.broadcasted_iota(jnp.int32, (L, d_attn), 1)
            rows = []
            for h in range(n_heads):
                for cc in range(2):
                    lo = h * V_DIM + cc * HEAD_DIM
                    rows.append(jnp.where((lane >= lo) & (lane < lo + HEAD_DIM), q, 0.0))
            qbd_ref[...] = jnp.concatenate(rows, axis=0).astype(BF16)
            dm_ref[...] = jnp.full(dm_ref.shape, -jnp.inf, F32)
            dl_ref[...] = jnp.zeros(dl_ref.shape, F32)
            dacc_ref[...] = jnp.zeros(dacc_ref.shape, F32)

        return slot, bd, c

    def decode_chunk(slot, bd, c):
        @pl.when(c < n_chunks - 1)
        def _():
            decode_update(decode_scores(decode_rows(kbuf, slot)) + dbias_ref[c], decode_rows(vbuf, slot))

        @pl.when(c == n_chunks - 1)
        def _():
            pad = jnp.zeros((LANES - L, d_attn), BF16)
            k_all = jnp.concatenate([decode_rows(kbuf, slot), sk_ref[bd].astype(BF16), pad], axis=0)
            v_all = jnp.concatenate([decode_rows(vbuf, slot), sv_ref[bd].astype(BF16), pad], axis=0)
            bias = jnp.concatenate([dbias_ref[c], dbiasn_ref[...]], axis=1)
            decode_update(decode_scores(k_all) + bias, v_all)
            out = dacc_ref[...] / dl_ref[...]
            heads = []
            for h in range(n_heads):
                r0 = 2 * h * L
                cols = slice(h * V_DIM, (h + 1) * V_DIM)
                heads.append(out[r0:r0 + L, cols] - lam * out[r0 + L:r0 + 2 * L, cols])
            o_ref[bd] = jnp.concatenate(heads, axis=-1)

    def decode_step(t):
        slot, bd, dc = decode_pre(t)
        decode_chunk(slot, bd, dc)

    @pl.when((b == 0) & (qi == 0))
    def _():
        for t0 in range(min(DECODE_SLOTS - 1, n_dec)):
            for cp in page_copies(t0, t0):
                cp.start()

    q = q_ref[...].astype(F32)
    lane = lax.broadcasted_iota(jnp.int32, (T, V_DIM), 1)
    qq = []
    for h in range(n_heads):
        qh = q[:, h * V_DIM:(h + 1) * V_DIM]
        qq.append(jnp.concatenate([jnp.where(lane < HEAD_DIM, qh, 0.0),
                                   jnp.where(lane >= HEAD_DIM, qh, 0.0)], axis=0).astype(BF16))

    m_ref[...] = jnp.full(m_ref.shape, -jnp.inf, F32)
    acc_ref[...] = jnp.zeros(acc_ref.shape, F32)

    def scores(j, h):
        kh = kb_ref[pl.ds(pl.multiple_of(j * T, T), T), :][:, h * V_DIM:(h + 1) * V_DIM]
        return lax.dot_general(kh, qq[h], NT_DIMS, preferred_element_type=F32)

    tiles_per_batch = nq * (nq + 1) // 2
    tile_base = b * tiles_per_batch + lax.div(qi * (qi + 1), 2)

    def kv_tile(j, near):
        vt = vt_ref[j]
        s_next = s0_ref[...]
        for h in range(n_heads):
            s12 = s_next
            s_next = scores(j, h + 1) if h + 1 < n_heads else scores(jnp.minimum(j + 1, qi), 0)
            vth = vt[h * VT_ROWS:(h + 1) * VT_ROWS, :]
            if near:
                bias = bias_ref[qi - j, h]
            for c in range(2):
                i = 2 * h + c
                s = s12[:, c * T:(c + 1) * T]
                if near:
                    s = s + bias
                m_old = m_ref[i]
                m_new = jnp.maximum(m_old, jnp.max(s, axis=0, keepdims=True))
                a = jnp.exp2(m_old - m_new)
                p = jnp.exp2(s - m_new)
                acc_ref[i] = a * acc_ref[i] + jnp.dot(vth, p.astype(BF16), preferred_element_type=F32)
                m_ref[i] = m_new
        s0_ref[...] = s_next

    def decode_for(u, width):
        t = lax.div(u + (DECODE_EVERY - 1), DECODE_EVERY)

        @pl.when((t * DECODE_EVERY < u + width) & (t < n_dec))
        def _():
            decode_step(t)

    def far_pair_body(jj, carry):
        j = FAR_UNROLL * jj
        decode_for(tile_base + j, FAR_UNROLL)
        for r in range(FAR_UNROLL):
            kv_tile(j + r, near=False)
        return carry

    def far_body(j, carry):
        decode_for(tile_base + j, 1)
        kv_tile(j, near=False)
        return carry

    def near_body(j, carry):
        decode_for(tile_base + j, 1)
        kv_tile(j, near=True)
        return carry

    @pl.when(qi >= 0)
    def _():
        s0_ref[...] = scores(0, 0)

    n_far = jnp.maximum(qi - 1, 0)
    n_groups = lax.div(n_far, FAR_UNROLL)
    lax.fori_loop(0, n_groups, far_pair_body, 0)
    lax.fori_loop(n_groups * FAR_UNROLL, n_far, far_body, 0)
    lax.fori_loop(n_far, qi + 1, near_body, 0)

    def attended(i):
        acc = acc_ref[i]
        return acc[:V_DIM] / acc[V_DIM:V_DIM + 1]

    normed = []
    for h in range(n_heads):
        ot = attended(2 * h) - lam * attended(2 * h + 1)
        ms = jnp.mean(ot * ot, axis=0, keepdims=True)
        normed.append((ot * lax.rsqrt(ms + SUBLN_EPS)).T * g_ref[...] * (1.0 - lam_init))
    y_ref[...] = _finish_rows(normed, sz_ref[...], gp_ref[...], x_ref[...], wout_ref, lng_ref, lnb_ref, alpha)

    first_left = -(-(n_batch * tiles_per_batch) // DECODE_EVERY)
    if first_left < n_dec:
        @pl.when((b == n_batch - 1) & (qi == nq - 1))
        def _():
            def tail_body(t, carry):
                decode_step(t)
                return carry
            lax.fori_loop(first_left, n_dec, tail_body, 0)


def _attn(page_table, q, kb, vt, bias_tab, lams, gp, sz, x, w_out, subln_g, ln_g, ln_b,
          qs, sk, sv, cache_k, cache_v, bias_past, bias_new, lam_init, alpha, n_heads):
    B, S, D = x.shape
    T = ATTN_TILE
    assert S % T == 0 and T >= MAX_DISTANCE
    nq = S // T
    d_attn = q.shape[-1]
    DB, L, _ = qs.shape
    n_pages = page_table.shape[1]
    rows_per_page = cache_k.shape[1]
    page = rows_per_page // n_heads
    P = DECODE_PAGES
    assert n_pages % P == 0 and L == SUBLANES
    n_chunks = n_pages // P
    n_rows = 2 * n_heads * L
    row_blk = lambda width: pl.BlockSpec((None, T, width), lambda b, i: (b, i, 0))
    seq_blk = pl.BlockSpec((None, S, d_attn), lambda b, i: (b, 0, 0), pipeline_mode=pl.Buffered(1))
    vt_blk = pl.BlockSpec((None, nq, n_heads * VT_ROWS, T), lambda b, i: (b, 0, 0, 0),
                          pipeline_mode=pl.Buffered(1))
    const = lambda a: pl.BlockSpec(a.shape, lambda b, i: (0,) * a.ndim)
    any_spec = pl.BlockSpec(memory_space=pl.ANY)
    in_specs = ([pl.BlockSpec(memory_space=pltpu.SMEM), row_blk(d_attn), seq_blk, vt_blk, const(bias_tab)]
                + [const(l) for l in lams]
                + [row_blk(gp.shape[-1]), row_blk(d_attn), row_blk(D), const(w_out), const(subln_g),
                   const(ln_g), const(ln_b),
                   const(qs), const(sk), const(sv), const(bias_past), const(bias_new), any_spec, any_spec])
    y, o = pl.pallas_call(
        functools.partial(_attn_kernel, T=T, n_heads=n_heads, lam_init=lam_init, alpha=alpha, P=P, page=page,
                          L=L, n_chunks=n_chunks, n_dec=DB * n_chunks, nq=nq, n_batch=B),
        grid=(B, nq),
        in_specs=in_specs,
        out_specs=(row_blk(D), const(qs)),
        out_shape=(jax.ShapeDtypeStruct((B, S, D), F32), jax.ShapeDtypeStruct((DB, L, d_attn), F32)),
        scratch_shapes=[pltpu.VMEM((2 * n_heads, 1, T), F32),
                        pltpu.VMEM((2 * n_heads, VT_ROWS, T), F32), pltpu.VMEM((T, 2 * T), F32),
                        pltpu.VMEM((DECODE_SLOTS, P, rows_per_page, V_DIM), F32),
                        pltpu.VMEM((DECODE_SLOTS, P, rows_per_page, V_DIM), F32),
                        pltpu.SemaphoreType.DMA((2, DECODE_SLOTS)),
                        pltpu.VMEM((n_rows, d_attn), BF16), pltpu.VMEM((n_rows, 1), F32),
                        pltpu.VMEM((n_rows, 1), F32), pltpu.VMEM((n_rows, d_attn), F32)],
        compiler_params=_compiler_params(2),
        name="attn",
    )(page_table.reshape(-1), q, kb, vt, bias_tab, *lams, gp, sz, x, w_out, subln_g, ln_g, ln_b,
      qs, sk, sv, bias_past, bias_new, cache_k, cache_v)
    return y, o


def _finish_sample_kernel(o_ref, sz_ref, gp_ref, x_ref, wout_ref, g_ref, lng_ref, lnb_ref, y_ref,
                          *, n_heads, lam_init, alpha):
    o = o_ref[...]
    normed = []
    for h in range(n_heads):
        oh = o[:, h * V_DIM:(h + 1) * V_DIM]
        ms = jnp.mean(oh * oh, axis=-1, keepdims=True)
        normed.append(oh * lax.rsqrt(ms + SUBLN_EPS) * g_ref[...] * (1.0 - lam_init))
    y_ref[...] = _finish_rows(normed, sz_ref[...], gp_ref[...], x_ref[...], wout_ref, lng_ref, lnb_ref, alpha)


def _finish_sample(o, sz, gp, x2d, w_out, subln_g, ln_g, ln_b, lam_init, alpha, n_heads):
    rows_total, D = x2d.shape
    rows = min(PROJ_ROWS, rows_total)
    assert rows_total % rows == 0
    row_blk = lambda a: pl.BlockSpec((rows, a.shape[-1]), lambda i: (i, 0))
    const = lambda a: pl.BlockSpec(a.shape, lambda i: (0,) * a.ndim)
    return pl.pallas_call(
        functools.partial(_finish_sample_kernel, n_heads=n_heads, lam_init=lam_init, alpha=alpha),
        grid=(rows_total // rows,),
        in_specs=[row_blk(o), row_blk(sz), row_blk(gp), row_blk(x2d), const(w_out), const(subln_g),
                  const(ln_g), const(ln_b)],
        out_specs=row_blk(x2d),
        out_shape=jax.ShapeDtypeStruct((rows_total, D), F32),
        compiler_params=_compiler_params(1),
        name="finish_sample",
    )(o, sz, gp, x2d, w_out, subln_g, ln_g, ln_b)


_N_LOG_BUCKETS = NUM_BUCKETS - MAX_EXACT
_LOG_BUCKET_STARTS = tuple(math.ceil(MAX_EXACT * (MAX_DISTANCE / MAX_EXACT) ** (k / _N_LOG_BUCKETS))
                           for k in range(1, _N_LOG_BUCKETS))


def _bucket(dist):
    large = MAX_EXACT + sum((dist >= st).astype(jnp.int32) for st in _LOG_BUCKET_STARTS)
    return jnp.where(dist < MAX_EXACT, dist, large)


def _bias_of_dist(dist, table):
    bucket = _bucket(jnp.maximum(dist, 0))[None]
    t = table.astype(F32)
    b = jnp.zeros((t.shape[1],) + dist.shape, F32)
    for i in range(NUM_BUCKETS):
        b = jnp.where(bucket == i, t[i][:, None, None], b)
    return jnp.where((dist >= 0)[None], b, NEG_INF)


def _prompt_bias_tiles(table, T):
    i = jnp.arange(T, dtype=jnp.int32)
    base = i[None, :] - i[:, None]
    far = _bias_of_dist(jnp.full((1, 1), MAX_DISTANCE, jnp.int32), table)
    return jnp.stack([(_bias_of_dist(base + off * T, table) - far) * LOG2E for off in range(2)])


def kernel(x_prompt, x_sample, cache_k, cache_v, state_pool, page_table, w_in, pool_w, pool_scale,
           lambda_q1, lambda_k1, lambda_q2, lambda_k2, subln_g, rel_bias, w_out, ln_g, ln_b):
    B, S, D = x_prompt.shape
    DB, L, _ = x_sample.shape
    depth = w_in.shape[0]
    n_pool_pages, page, n_heads, hd2 = cache_k.shape[1:]
    assert hd2 == V_DIM
    n_pages = page_table.shape[1]
    past_len = n_pages * page
    d_attn = n_heads * V_DIM
    d_pool = pool_scale.shape[-1]
    assert d_pool == len(POOL_WINDOWS) * LANES and w_in.shape[-1] == 2 * d_pool + 4 * d_attn
    alpha = (2 * depth) ** 0.25

    bias_tab = _prompt_bias_tiles(rel_bias, ATTN_TILE)
    qpos_s = past_len + jnp.arange(L, dtype=jnp.int32)
    kpos_s = jnp.arange(past_len + L, dtype=jnp.int32)
    bias_s = _bias_of_dist(qpos_s[:, None] - kpos_s[None, :], rel_bias)
    bias_s = jnp.broadcast_to(bias_s[:, None], (n_heads, 2, L, past_len + L)).reshape(2 * n_heads * L, -1)
    bias_past = bias_s[:, :past_len].reshape(bias_s.shape[0], -1, DECODE_PAGES * page).transpose(1, 0, 2)
    bias_new = jnp.pad(bias_s[:, past_len:], ((0, 0), (0, LANES - L)), constant_values=NEG_INF)

    xp = x_prompt
    xs = x_sample.reshape(DB * L, D)
    outs = [[] for _ in range(6)]
    for layer in range(depth):
        lam_init = _lambda_init(layer)
        w_in_b = w_in[layer].astype(BF16)
        w_out_b = w_out[layer].astype(BF16)
        pool_w_b = pool_w[layer].astype(BF16)
        ps = pool_scale[layer][None]
        lams = [v[layer][None] for v in (lambda_q1, lambda_k1, lambda_q2, lambda_k2)]
        g, lg, lb = subln_g[layer][None], ln_g[layer][None], ln_b[layer][None]
        ck = cache_k[layer].reshape(n_pool_pages, page * n_heads, V_DIM)
        cv = cache_v[layer].reshape(n_pool_pages, page * n_heads, V_DIM)

        k_p, v_p, q_p, kb_p, vt_p, gp_p, sz_p, pp = _proj_prompt(xp, w_in_b, pool_w_b, ps, d_pool, d_attn, n_heads)
        k_s, v_s, q_s, gp_s, sz_s, ps_new = _proj_sample(xs, state_pool[layer], w_in_b, pool_w_b, ps, L,
                                                        past_len, d_pool, d_attn, n_heads)
        yp, o_s = _attn(page_table, q_p, kb_p, vt_p, bias_tab, lams, gp_p, sz_p, xp, w_out_b, g, lg, lb,
                        q_s.reshape(DB, L, d_attn), k_s.reshape(DB, L, d_attn), v_s.reshape(DB, L, d_attn),
                        ck, cv, bias_past, bias_new, lam_init, alpha, n_heads)
        ys = _finish_sample(o_s.reshape(DB * L, d_attn), sz_s, gp_s, xs, w_out_b, g, lg, lb,
                            lam_init, alpha, n_heads)

        outs[0].append(k_p.reshape(B, S, n_heads, V_DIM))
        outs[1].append(v_p.reshape(B, S, n_heads, V_DIM))
        outs[2].append(pp[:, HALO - POOL_BUF:])
        outs[3].append(k_s.reshape(DB, L, n_heads, V_DIM))
        outs[4].append(v_s.reshape(DB, L, n_heads, V_DIM))
        outs[5].append(ps_new)
        xp, xs = yp, ys

    return (xp, xs.reshape(DB, L, D), *(jnp.stack(o) for o in outs))
```

```python
import functools
import math

import jax
import jax.numpy as jnp
from jax import lax
from jax.experimental import pallas as pl
from jax.experimental.pallas import tpu as pltpu

F32 = jnp.float32
BF16 = jnp.bfloat16

POOL_WINDOWS = (2, 4, 8, 16)
POOL_BUF = max(POOL_WINDOWS) - 1
HEAD_DIM = 64
V_DIM = 2 * HEAD_DIM
ATTN_SCALE = HEAD_DIM ** -0.5
LOG2E = math.log2(math.e)
NUM_BUCKETS = 32
MAX_DISTANCE = 128
MAX_EXACT = NUM_BUCKETS // 2
NEG_INF = -1e30
LN_EPS = 1e-5
SUBLN_EPS = 1e-5

LANES = 128
SUBLANES = 8
HALO = 2 * SUBLANES
VT_PAD = 2 * SUBLANES
VT_ROWS = V_DIM + VT_PAD
VMEM_LIMIT_BYTES = 56 * 1024 * 1024

PROJ_ROWS = 1024
ATTN_TILE = 256
DECODE_PAGES = 16
DECODE_EVERY = 2
DECODE_SLOTS = 2
NT_DIMS = (((1,), (1,)), ((), ()))


def _lambda_init(layer):
    return 0.8 - 0.6 * math.exp(-0.3 * layer)


def _silu(z):
    return z / (1.0 + jnp.exp(-z))


def _lam_value(lq1, lk1, lq2, lk2, lam_init):
    a = jnp.sum(lq1 * lk1, axis=-1, keepdims=True)
    b = jnp.sum(lq2 * lk2, axis=-1, keepdims=True)
    return jnp.exp(a) - jnp.exp(b) + lam_init


def _compiler_params(n_axes):
    return pltpu.CompilerParams(dimension_semantics=("arbitrary",) * n_axes,
                                vmem_limit_bytes=VMEM_LIMIT_BYTES)


def _pool_diffs(u, window_sum, cnt_of):
    diffs = []
    for g, w in enumerate(POOL_WINDOWS):
        sl = slice(g * LANES, (g + 1) * LANES)
        diffs.append((window_sum(g, w) / cnt_of(w) - u[:, sl]).astype(BF16))
    return diffs


def _pool_mix(diffs, pw_ref, ps_ref):
    outs = [jnp.dot(d, pw_ref[g], preferred_element_type=F32) for g, d in enumerate(diffs)]
    return jnp.concatenate(outs, axis=-1) * ps_ref[...]


def _store_heads_interleaved(dst_ref, val, rows, n_heads):
    for h in range(n_heads):
        dst_ref[pl.ds(h, rows, stride=n_heads), :] = val[:, h * V_DIM:(h + 1) * V_DIM]


def _proj_prompt_kernel(x_ref, w_ref, pw_ref, ps_ref,
                        k_ref, v_ref, q_ref, kb_ref, vt_ref, gp_ref, sz_ref, pp_ref,
                        ext_ref, *, bm, d_pool, d_attn, n_heads):
    s = pl.program_id(1)
    xb = x_ref[...].astype(BF16)
    offs = [0, d_pool, 2 * d_pool, 2 * d_pool + d_attn, 2 * d_pool + 2 * d_attn,
            2 * d_pool + 3 * d_attn, 2 * d_pool + 4 * d_attn]

    def proj(j):
        return jnp.dot(xb, w_ref[:, offs[j]:offs[j + 1]], preferred_element_type=F32)

    pu = proj(0)

    @pl.when(s == 0)
    def _():
        ext_ref[0:HALO, :] = jnp.zeros((HALO, d_pool), F32)

    ext_ref[HALO:HALO + bm, :] = pu
    pos = s * bm + lax.broadcasted_iota(jnp.int32, (bm, 1), 0)

    def window_sum(g, w):
        sl = slice(g * LANES, (g + 1) * LANES)
        acc = pu[:, sl]
        for sh in range(1, w):
            acc = acc + ext_ref[HALO - sh:HALO - sh + bm, sl]
        return acc

    def cnt_of(w):
        return jnp.minimum(pos + 1, w).astype(F32)

    diffs = _pool_diffs(pu, window_sum, cnt_of)
    ext_ref[0:HALO, :] = pu[bm - HALO:, :]
    gate = _silu(proj(1))
    q_ref[...] = (proj(2) * (ATTN_SCALE * LOG2E)).astype(BF16)
    k = proj(3)
    kb_ref[...] = k.astype(BF16)
    _store_heads_interleaved(k_ref, k, bm, n_heads)
    gp_ref[...] = (_pool_mix(diffs, pw_ref, ps_ref) * gate).astype(BF16)
    v = proj(4)
    ones_rows = (lax.broadcasted_iota(jnp.int32, (VT_PAD, ATTN_TILE), 0) == 0).astype(BF16)
    for t in range(bm // ATTN_TILE):
        v_tile = v[t * ATTN_TILE:(t + 1) * ATTN_TILE, :]
        for h in range(n_heads):
            r0 = h * VT_ROWS
            vt_ref[t, r0:r0 + V_DIM, :] = v_tile[:, h * V_DIM:(h + 1) * V_DIM].T.astype(BF16)
            vt_ref[t, r0 + V_DIM:r0 + VT_ROWS, :] = ones_rows
    _store_heads_interleaved(v_ref, v, bm, n_heads)
    sz_ref[...] = _silu(proj(5))

    @pl.when(s == pl.num_programs(1) - 1)
    def _():
        pp_ref[...] = pu[bm - HALO:, :]


def _proj_prompt(x, w_in, pool_w, pool_scale, d_pool, d_attn, n_heads):
    B, S, D = x.shape
    bm = PROJ_ROWS
    T = ATTN_TILE
    assert S % bm == 0 and bm >= HALO and bm % T == 0
    d_in = w_in.shape[1]
    row_blk = lambda width: pl.BlockSpec((None, bm, width), lambda b, s: (b, s, 0))
    const2 = lambda shape: pl.BlockSpec(shape, lambda b, s: (0,) * len(shape))
    out_shape = (
        jax.ShapeDtypeStruct((B, S * n_heads, V_DIM), F32),
        jax.ShapeDtypeStruct((B, S * n_heads, V_DIM), F32),
        jax.ShapeDtypeStruct((B, S, d_attn), BF16),
        jax.ShapeDtypeStruct((B, S, d_attn), BF16),
        jax.ShapeDtypeStruct((B, S // T, n_heads * VT_ROWS, T), BF16),
        jax.ShapeDtypeStruct((B, S, d_pool), BF16),
        jax.ShapeDtypeStruct((B, S, d_attn), F32),
        jax.ShapeDtypeStruct((B, HALO, d_pool), F32),
    )
    kv_blk = pl.BlockSpec((None, bm * n_heads, V_DIM), lambda b, s: (b, s, 0))
    vt_blk = pl.BlockSpec((None, bm // T, n_heads * VT_ROWS, T), lambda b, s: (b, s, 0, 0))
    out_specs = (kv_blk, kv_blk, row_blk(d_attn), row_blk(d_attn), vt_blk,
                 row_blk(d_pool), row_blk(d_attn),
                 pl.BlockSpec((None, HALO, d_pool), lambda b, s: (b, 0, 0)))
    return pl.pallas_call(
        functools.partial(_proj_prompt_kernel, bm=bm, d_pool=d_pool, d_attn=d_attn, n_heads=n_heads),
        grid=(B, S // bm),
        in_specs=[row_blk(D), const2((D, d_in)), const2(pool_w.shape), const2(pool_scale.shape)],
        out_specs=out_specs,
        out_shape=out_shape,
        scratch_shapes=[pltpu.VMEM((HALO + bm, d_pool), F32)],
        compiler_params=_compiler_params(2),
        name="proj_prompt",
    )(x, w_in, pool_w, pool_scale)


def _proj_sample_kernel(x_ref, st_ref, w_ref, pw_ref, ps_ref,
                        k_ref, v_ref, q_ref, gp_ref, sz_ref, ps_out_ref,
                        ext_ref, *, nb, L, pos0, d_pool, d_attn, n_heads):
    rows = nb * L
    xb = x_ref[...].astype(BF16)
    offs = [0, d_pool, 2 * d_pool, 2 * d_pool + d_attn, 2 * d_pool + 2 * d_attn,
            2 * d_pool + 3 * d_attn, 2 * d_pool + 4 * d_attn]

    def proj(j):
        return jnp.dot(xb, w_ref[:, offs[j]:offs[j + 1]], preferred_element_type=F32)

    su = proj(0)
    ext_ref[:, HALO - POOL_BUF:HALO, :] = st_ref[...]
    ext_ref[:, HALO:HALO + L, :] = su.reshape(nb, L, d_pool)
    pos = pos0 + lax.rem(lax.broadcasted_iota(jnp.int32, (rows, 1), 0), L)

    def window_sum(g, w):
        sl = slice(g * LANES, (g + 1) * LANES)
        acc = ext_ref[:, HALO:HALO + L, sl]
        for sh in range(1, w):
            acc = acc + ext_ref[:, HALO - sh:HALO - sh + L, sl]
        return acc.reshape(rows, LANES)

    def cnt_of(w):
        return jnp.minimum(pos + 1, w).astype(F32)

    pool_y = _pool_mix(_pool_diffs(su, window_sum, cnt_of), pw_ref, ps_ref)
    ps_out_ref[...] = ext_ref[:, HALO + L - POOL_BUF:HALO + L, :]

    gp_ref[...] = (pool_y * _silu(proj(1))).astype(BF16)
    q_ref[...] = proj(2) * ATTN_SCALE
    _store_heads_interleaved(k_ref, proj(3), rows, n_heads)
    _store_heads_interleaved(v_ref, proj(4), rows, n_heads)
    sz_ref[...] = _silu(proj(5))


def _proj_sample(x2d, state, w_in, pool_w, pool_scale, L, pos0, d_pool, d_attn, n_heads):
    rows_total, D = x2d.shape
    DB = state.shape[0]
    assert L == SUBLANES and rows_total == DB * L
    rows = min(PROJ_ROWS, rows_total)
    nb = rows // L
    assert rows_total % rows == 0
    d_in = w_in.shape[1]
    row_blk = lambda width: pl.BlockSpec((rows, width), lambda i: (i, 0))
    const = lambda shape: pl.BlockSpec(shape, lambda i: (0,) * len(shape))
    st_blk = pl.BlockSpec((nb, POOL_BUF, d_pool), lambda i: (i, 0, 0))
    kv_blk = pl.BlockSpec((rows * n_heads, V_DIM), lambda i: (i, 0))
    out_shape = (
        jax.ShapeDtypeStruct((rows_total * n_heads, V_DIM), F32),
        jax.ShapeDtypeStruct((rows_total * n_heads, V_DIM), F32),
        jax.ShapeDtypeStruct((rows_total, d_attn), F32),
        jax.ShapeDtypeStruct((rows_total, d_pool), BF16),
        jax.ShapeDtypeStruct((rows_total, d_attn), F32),
        jax.ShapeDtypeStruct((DB, POOL_BUF, d_pool), F32),
    )
    return pl.pallas_call(
        functools.partial(_proj_sample_kernel, nb=nb, L=L, pos0=pos0, d_pool=d_pool, d_attn=d_attn,
                          n_heads=n_heads),
        grid=(rows_total // rows,),
        in_specs=[row_blk(D), st_blk, const((D, d_in)), const(pool_w.shape), const(pool_scale.shape)],
        out_specs=(kv_blk, kv_blk, row_blk(d_attn), row_blk(d_pool), row_blk(d_attn), st_blk),
        out_shape=out_shape,
        scratch_shapes=[pltpu.VMEM((nb, HALO + L, d_pool), F32)],
        compiler_params=_compiler_params(1),
        name="proj_sample",
    )(x2d, state, w_in, pool_w, pool_scale)


def _finish_rows(normed_heads, sz, gp, x, wout_ref, lng_ref, lnb_ref, alpha):
    o = jnp.concatenate(normed_heads, axis=-1) * sz
    gated = jnp.concatenate([gp, o.astype(BF16)], axis=-1)
    h = jnp.dot(gated, wout_ref[...], preferred_element_type=F32)
    r = alpha * x + h
    mu = jnp.mean(r, axis=-1, keepdims=True)
    rc = r - mu
    var = jnp.mean(rc * rc, axis=-1, keepdims=True)
    return rc * lax.rsqrt(var + LN_EPS) * lng_ref[...] + lnb_ref[...]


def _deinterleave_heads(page_ref, rows, n_heads):
    return jnp.concatenate(
        [page_ref[pl.ds(h, rows, stride=n_heads), :].astype(BF16) for h in range(n_heads)], axis=-1)


def _attn_kernel(pt_ref, q_ref, kb_ref, vt_ref, bias_ref, lq1_ref, lk1_ref, lq2_ref, lk2_ref,
                 gp_ref, sz_ref, x_ref, wout_ref, g_ref, lng_ref, lnb_ref,
                 qs_ref, sk_ref, sv_ref, dbias_ref, dbiasn_ref, ck_hbm, cv_hbm,
                 y_ref, o_ref,
                 m_ref, acc_ref, s0_ref, kbuf, vbuf, sem, qbd_ref, dm_ref, dl_ref, dacc_ref,
                 *, T, n_heads, lam_init, alpha, P, page, L, n_chunks, n_dec, nq, n_batch):
    b = pl.program_id(0)
    qi = pl.program_id(1)
    d_attn = n_heads * V_DIM
    lam = _lam_value(lq1_ref[...], lk1_ref[...], lq2_ref[...], lk2_ref[...], lam_init)

    def page_copies(t, slot):
        copies = []
        for i in range(P):
            pg = pt_ref[t * P + i]
            copies.append(pltpu.make_async_copy(ck_hbm.at[pg], kbuf.at[slot, i], sem.at[0, slot]))
            copies.append(pltpu.make_async_copy(cv_hbm.at[pg], vbuf.at[slot, i], sem.at[1, slot]))
        return copies

    def decode_rows(buf, slot):
        return jnp.concatenate([_deinterleave_heads(buf.at[slot, i], page, n_heads) for i in range(P)], axis=0)

    def decode_scores(k_rows):
        return lax.dot_general(qbd_ref[...], k_rows, NT_DIMS, preferred_element_type=F32)

    def decode_update(s, v_rows):
        m_old = dm_ref[...]
        m_new = jnp.maximum(m_old, jnp.max(s, axis=-1, keepdims=True))
        a = jnp.exp(m_old - m_new)
        p = jnp.exp(s - m_new)
        dl_ref[...] = a * dl_ref[...] + jnp.sum(p, axis=-1, keepdims=True)
        dacc_ref[...] = a * dacc_ref[...] + jnp.dot(p.astype(BF16), v_rows, preferred_element_type=F32)
        dm_ref[...] = m_new

    def decode_pre(t):
        slot = lax.rem(t, DECODE_SLOTS)
        ahead = t + (DECODE_SLOTS - 1)

        @pl.when(ahead < n_dec)
        def _():
            for cp in page_copies(ahead, lax.rem(ahead, DECODE_SLOTS)):
                cp.start()

        for cp in page_copies(t, slot):
            cp.wait()

        bd = lax.div(t, n_chunks)
        c = lax.rem(t, n_chunks)

        @pl.when(c == 0)
        def _():
            q = qs_ref[bd]
            lane = lax.broadcasted_iota(jnp.int32, (L, d_attn), 1)
            rows = []
            for h in range(n_heads):
                for cc in range(2):
                    lo = h * V_DIM + cc * HEAD_DIM
                    rows.append(jnp.where((lane >= lo) & (lane < lo + HEAD_DIM), q, 0.0))
            qbd_ref[...] = jnp.concatenate(rows, axis=0).astype(BF16)
            dm_ref[...] = jnp.full(dm_ref.shape, -jnp.inf, F32)
            dl_ref[...] = jnp.zeros(dl_ref.shape, F32)
            dacc_ref[...] = jnp.zeros(dacc_ref.shape, F32)

        return slot, bd, c

    def decode_chunk(slot, bd, c):
        @pl.when(c < n_chunks - 1)
        def _():
            decode_update(decode_scores(decode_rows(kbuf, slot)) + dbias_ref[c], decode_rows(vbuf, slot))

        @pl.when(c == n_chunks - 1)
        def _():
            pad = jnp.zeros((LANES - L, d_attn), BF16)
            k_all = jnp.concatenate([decode_rows(kbuf, slot), sk_ref[bd].astype(BF16), pad], axis=0)
            v_all = jnp.concatenate([decode_rows(vbuf, slot), sv_ref[bd].astype(BF16), pad], axis=0)
            bias = jnp.concatenate([dbias_ref[c], dbiasn_ref[...]], axis=1)
            decode_update(decode_scores(k_all) + bias, v_all)
            out = dacc_ref[...] / dl_ref[...]
            heads = []
            for h in range(n_heads):
                r0 = 2 * h * L
                cols = slice(h * V_DIM, (h + 1) * V_DIM)
                heads.append(out[r0:r0 + L, cols] - lam * out[r0 + L:r0 + 2 * L, cols])
            o_ref[bd] = jnp.concatenate(heads, axis=-1)

    def decode_step(t):
        slot, bd, dc = decode_pre(t)
        decode_chunk(slot, bd, dc)

    @pl.when((b == 0) & (qi == 0))
    def _():
        for t0 in range(min(DECODE_SLOTS - 1, n_dec)):
            for cp in page_copies(t0, t0):
                cp.start()

    q = q_ref[...].astype(F32)
    lane = lax.broadcasted_iota(jnp.int32, (T, V_DIM), 1)
    qq = []
    for h in range(n_heads):
        qh = q[:, h * V_DIM:(h + 1) * V_DIM]
        qq.append(jnp.concatenate([jnp.where(lane < HEAD_DIM, qh, 0.0),
                                   jnp.where(lane >= HEAD_DIM, qh, 0.0)], axis=0).astype(BF16))

    m_ref[...] = jnp.full(m_ref.shape, -jnp.inf, F32)
    acc_ref[...] = jnp.zeros(acc_ref.shape, F32)

    def scores(j, h):
        kh = kb_ref[pl.ds(pl.multiple_of(j * T, T), T), :][:, h * V_DIM:(h + 1) * V_DIM]
        return lax.dot_general(kh, qq[h], NT_DIMS, preferred_element_type=F32)

    tiles_per_batch = nq * (nq + 1) // 2
    tile_base = b * tiles_per_batch + lax.div(qi * (qi + 1), 2)

    def kv_tile(j, near):
        u = tile_base + j
        t = lax.div(u, DECODE_EVERY)

        @pl.when((lax.rem(u, DECODE_EVERY) == 0) & (t < n_dec))
        def _():
            decode_step(t)

        vt = vt_ref[j]
        s_next = s0_ref[...]
        for h in range(n_heads):
            s12 = s_next
            s_next = scores(j, h + 1) if h + 1 < n_heads else scores(jnp.minimum(j + 1, qi), 0)
            vth = vt[h * VT_ROWS:(h + 1) * VT_ROWS, :]
            if near:
                bias = bias_ref[qi - j, h]
            for c in range(2):
                i = 2 * h + c
                s = s12[:, c * T:(c + 1) * T]
                if near:
                    s = s + bias
                m_old = m_ref[i]
                m_new = jnp.maximum(m_old, jnp.max(s, axis=0, keepdims=True))
                a = jnp.exp2(m_old - m_new)
                p = jnp.exp2(s - m_new)
                acc_ref[i] = a * acc_ref[i] + jnp.dot(vth, p.astype(BF16), preferred_element_type=F32)
                m_ref[i] = m_new
        s0_ref[...] = s_next

    def far_body(j, carry):
        kv_tile(j, near=False)
        return carry

    def near_body(j, carry):
        kv_tile(j, near=True)
        return carry

    @pl.when(qi >= 0)
    def _():
        s0_ref[...] = scores(0, 0)

    n_far = jnp.maximum(qi - 1, 0)
    lax.fori_loop(0, n_far, far_body, 0)
    lax.fori_loop(n_far, qi + 1, near_body, 0)

    def attended(i):
        acc = acc_ref[i]
        return acc[:V_DIM] / acc[V_DIM:V_DIM + 1]

    normed = []
    for h in range(n_heads):
        ot = attended(2 * h) - lam * attended(2 * h + 1)
        ms = jnp.mean(ot * ot, axis=0, keepdims=True)
        normed.append((ot * lax.rsqrt(ms + SUBLN_EPS)).T * g_ref[...] * (1.0 - lam_init))
    y_ref[...] = _finish_rows(normed, sz_ref[...], gp_ref[...], x_ref[...], wout_ref, lng_ref, lnb_ref, alpha)

    first_left = -(-(n_batch * tiles_per_batch) // DECODE_EVERY)
    if first_left < n_dec:
        @pl.when((b == n_batch - 1) & (qi == nq - 1))
        def _():
            def tail_body(t, carry):
                decode_step(t)
                return carry
            lax.fori_loop(first_left, n_dec, tail_body, 0)


def _attn(page_table, q, kb, vt, bias_tab, lams, gp, sz, x, w_out, subln_g, ln_g, ln_b,
          qs, sk, sv, cache_k, cache_v, bias_past, bias_new, lam_init, alpha, n_heads):
    B, S, D = x.shape
    T = ATTN_TILE
    assert S % T == 0 and T >= MAX_DISTANCE
    nq = S // T
    d_attn = q.shape[-1]
    DB, L, _ = qs.shape
    n_pages = page_table.shape[1]
    rows_per_page = cache_k.shape[1]
    page = rows_per_page // n_heads
    P = DECODE_PAGES
    assert n_pages % P == 0 and L == SUBLANES
    n_chunks = n_pages // P
    n_rows = 2 * n_heads * L
    row_blk = lambda width: pl.BlockSpec((None, T, width), lambda b, i: (b, i, 0))
    seq_blk = pl.BlockSpec((None, S, d_attn), lambda b, i: (b, 0, 0), pipeline_mode=pl.Buffered(1))
    vt_blk = pl.BlockSpec((None, nq, n_heads * VT_ROWS, T), lambda b, i: (b, 0, 0, 0),
                          pipeline_mode=pl.Buffered(1))
    const = lambda a: pl.BlockSpec(a.shape, lambda b, i: (0,) * a.ndim)
    any_spec = pl.BlockSpec(memory_space=pl.ANY)
    in_specs = ([pl.BlockSpec(memory_space=pltpu.SMEM), row_blk(d_attn), seq_blk, vt_blk, const(bias_tab)]
                + [const(l) for l in lams]
                + [row_blk(gp.shape[-1]), row_blk(d_attn), row_blk(D), const(w_out), const(subln_g),
                   const(ln_g), const(ln_b),
                   const(qs), const(sk), const(sv), const(bias_past), const(bias_new), any_spec, any_spec])
    y, o = pl.pallas_call(
        functools.partial(_attn_kernel, T=T, n_heads=n_heads, lam_init=lam_init, alpha=alpha, P=P, page=page,
                          L=L, n_chunks=n_chunks, n_dec=DB * n_chunks, nq=nq, n_batch=B),
        grid=(B, nq),
        in_specs=in_specs,
        out_specs=(row_blk(D), const(qs)),
        out_shape=(jax.ShapeDtypeStruct((B, S, D), F32), jax.ShapeDtypeStruct((DB, L, d_attn), F32)),
        scratch_shapes=[pltpu.VMEM((2 * n_heads, 1, T), F32),
                        pltpu.VMEM((2 * n_heads, VT_ROWS, T), F32), pltpu.VMEM((T, 2 * T), F32),
                        pltpu.VMEM((DECODE_SLOTS, P, rows_per_page, V_DIM), F32),
                        pltpu.VMEM((DECODE_SLOTS, P, rows_per_page, V_DIM), F32),
                        pltpu.SemaphoreType.DMA((2, DECODE_SLOTS)),
                        pltpu.VMEM((n_rows, d_attn), BF16), pltpu.VMEM((n_rows, 1), F32),
                        pltpu.VMEM((n_rows, 1), F32), pltpu.VMEM((n_rows, d_attn), F32)],
        compiler_params=_compiler_params(2),
        name="attn",
    )(page_table.reshape(-1), q, kb, vt, bias_tab, *lams, gp, sz, x, w_out, subln_g, ln_g, ln_b,
      qs, sk, sv, bias_past, bias_new, cache_k, cache_v)
    return y, o


def _finish_sample_kernel(o_ref, sz_ref, gp_ref, x_ref, wout_ref, g_ref, lng_ref, lnb_ref, y_ref,
                          *, n_heads, lam_init, alpha):
    o = o_ref[...]
    normed = []
    for h in range(n_heads):
        oh = o[:, h * V_DIM:(h + 1) * V_DIM]
        ms = jnp.mean(oh * oh, axis=-1, keepdims=True)
        normed.append(oh * lax.rsqrt(ms + SUBLN_EPS) * g_ref[...] * (1.0 - lam_init))
    y_ref[...] = _finish_rows(normed, sz_ref[...], gp_ref[...], x_ref[...], wout_ref, lng_ref, lnb_ref, alpha)


def _finish_sample(o, sz, gp, x2d, w_out, subln_g, ln_g, ln_b, lam_init, alpha, n_heads):
    rows_total, D = x2d.shape
    rows = min(PROJ_ROWS, rows_total)
    assert rows_total % rows == 0
    row_blk = lambda a: pl.BlockSpec((rows, a.shape[-1]), lambda i: (i, 0))
    const = lambda a: pl.BlockSpec(a.shape, lambda i: (0,) * a.ndim)
    return pl.pallas_call(
        functools.partial(_finish_sample_kernel, n_heads=n_heads, lam_init=lam_init, alpha=alpha),
        grid=(rows_total // rows,),
        in_specs=[row_blk(o), row_blk(sz), row_blk(gp), row_blk(x2d), const(w_out), const(subln_g),
                  const(ln_g), const(ln_b)],
        out_specs=row_blk(x2d),
        out_shape=jax.ShapeDtypeStruct((rows_total, D), F32),
        compiler_params=_compiler_params(1),
        name="finish_sample",
    )(o, sz, gp, x2d, w_out, subln_g, ln_g, ln_b)


_N_LOG_BUCKETS = NUM_BUCKETS - MAX_EXACT
_LOG_BUCKET_STARTS = tuple(math.ceil(MAX_EXACT * (MAX_DISTANCE / MAX_EXACT) ** (k / _N_LOG_BUCKETS))
                           for k in range(1, _N_LOG_BUCKETS))


def _bucket(dist):
    large = MAX_EXACT + sum((dist >= st).astype(jnp.int32) for st in _LOG_BUCKET_STARTS)
    return jnp.where(dist < MAX_EXACT, dist, large)


def _bias_of_dist(dist, table):
    bucket = _bucket(jnp.maximum(dist, 0))[None]
    t = table.astype(F32)
    b = jnp.zeros((t.shape[1],) + dist.shape, F32)
    for i in range(NUM_BUCKETS):
        b = jnp.where(bucket == i, t[i][:, None, None], b)
    return jnp.where((dist >= 0)[None], b, NEG_INF)


def _prompt_bias_tiles(table, T):
    i = jnp.arange(T, dtype=jnp.int32)
    base = i[None, :] - i[:, None]
    far = _bias_of_dist(jnp.full((1, 1), MAX_DISTANCE, jnp.int32), table)
    return jnp.stack([(_bias_of_dist(base + off * T, table) - far) * LOG2E for off in range(2)])


def kernel(x_prompt, x_sample, cache_k, cache_v, state_pool, page_table, w_in, pool_w, pool_scale,
           lambda_q1, lambda_k1, lambda_q2, lambda_k2, subln_g, rel_bias, w_out, ln_g, ln_b):
    B, S, D = x_prompt.shape
    DB, L, _ = x_sample.shape
    depth = w_in.shape[0]
    n_pool_pages, page, n_heads, hd2 = cache_k.shape[1:]
    assert hd2 == V_DIM
    n_pages = page_table.shape[1]
    past_len = n_pages * page
    d_attn = n_heads * V_DIM
    d_pool = pool_scale.shape[-1]
    assert d_pool == len(POOL_WINDOWS) * LANES and w_in.shape[-1] == 2 * d_pool + 4 * d_attn
    alpha = (2 * depth) ** 0.25

    bias_tab = _prompt_bias_tiles(rel_bias, ATTN_TILE)
    qpos_s = past_len + jnp.arange(L, dtype=jnp.int32)
    kpos_s = jnp.arange(past_len + L, dtype=jnp.int32)
    bias_s = _bias_of_dist(qpos_s[:, None] - kpos_s[None, :], rel_bias)
    bias_s = jnp.broadcast_to(bias_s[:, None], (n_heads, 2, L, past_len + L)).reshape(2 * n_heads * L, -1)
    bias_past = bias_s[:, :past_len].reshape(bias_s.shape[0], -1, DECODE_PAGES * page).transpose(1, 0, 2)
    bias_new = jnp.pad(bias_s[:, past_len:], ((0, 0), (0, LANES - L)), constant_values=NEG_INF)

    xp = x_prompt
    xs = x_sample.reshape(DB * L, D)
    outs = [[] for _ in range(6)]
    for layer in range(depth):
        lam_init = _lambda_init(layer)
        w_in_b = w_in[layer].astype(BF16)
        w_out_b = w_out[layer].astype(BF16)
        pool_w_b = pool_w[layer].astype(BF16)
        ps = pool_scale[layer][None]
        lams = [v[layer][None] for v in (lambda_q1, lambda_k1, lambda_q2, lambda_k2)]
        g, lg, lb = subln_g[layer][None], ln_g[layer][None], ln_b[layer][None]
        ck = cache_k[layer].reshape(n_pool_pages, page * n_heads, V_DIM)
        cv = cache_v[layer].reshape(n_pool_pages, page * n_heads, V_DIM)

        k_p, v_p, q_p, kb_p, vt_p, gp_p, sz_p, pp = _proj_prompt(xp, w_in_b, pool_w_b, ps, d_pool, d_attn, n_heads)
        k_s, v_s, q_s, gp_s, sz_s, ps_new = _proj_sample(xs, state_pool[layer], w_in_b, pool_w_b, ps, L,
                                                        past_len, d_pool, d_attn, n_heads)
        yp, o_s = _attn(page_table, q_p, kb_p, vt_p, bias_tab, lams, gp_p, sz_p, xp, w_out_b, g, lg, lb,
                        q_s.reshape(DB, L, d_attn), k_s.reshape(DB, L, d_attn), v_s.reshape(DB, L, d_attn),
                        ck, cv, bias_past, bias_new, lam_init, alpha, n_heads)
        ys = _finish_sample(o_s.reshape(DB * L, d_attn), sz_s, gp_s, xs, w_out_b, g, lg, lb,
                            lam_init, alpha, n_heads)

        outs[0].append(k_p.reshape(B, S, n_heads, V_DIM))
        outs[1].append(v_p.reshape(B, S, n_heads, V_DIM))
        outs[2].append(pp[:, HALO - POOL_BUF:])
        outs[3].append(k_s.reshape(DB, L, n_heads, V_DIM))
        outs[4].append(v_s.reshape(DB, L, n_heads, V_DIM))
        outs[5].append(ps_new)
        xp, xs = yp, ys

    return (xp, xs.reshape(DB, L, D), *(jnp.stack(o) for o in outs))
```
